```python
import jax, jax.numpy as jnp
from jax import lax
import numpy as np

D_MODEL = 2048
BATCH = 2
SEQ = 4096
DEPTH = 1

HEAD_DIM = 128
MIX_WIDTH = D_MODEL
N_HEADS_A = MIX_WIDTH // (2 * HEAD_DIM)
N_HEADS_B = MIX_WIDTH // (2 * HEAD_DIM)
GQA_REP = 4
N_KV_B = N_HEADS_B // GQA_REP
WIDTH_A = N_HEADS_A * HEAD_DIM
WIDTH_B = N_HEADS_B * HEAD_DIM
KV_WIDTH_B = N_KV_B * HEAD_DIM
DILATED_GROUPS = ((128, 1), (512, 4), (2048, 16))
HALF_WINDOW_B = 128
N_MEM = 256
N_HEADS_MEM = 4
WIDTH_MEM = N_HEADS_MEM * HEAD_DIM
N_EXPERTS = 16
EC_CAPACITY = 2
D_FF_EXPERT = D_MODEL
ROPE_THETA = 10000.0
EPS = 1e-6
NEG_INF = -1e30
IN_WIDTHS = (WIDTH_A, WIDTH_A, WIDTH_A, WIDTH_B, KV_WIDTH_B, KV_WIDTH_B)
IN_OFFSETS = tuple(int(o) for o in np.cumsum(IN_WIDTHS)[:-1])
D_IN = int(sum(IN_WIDTHS))

kernel_name = "hybrid_dilated_swa_sink_ec_moe_encoder"


def rmsnorm(x, g):
    xf = x.astype(jnp.float32)
    y = xf * lax.rsqrt(jnp.mean(xf * xf, axis=-1, keepdims=True) + EPS)
    return (y * g.astype(jnp.float32)).astype(x.dtype)


def rope_tables(positions):
    inv_freq = ROPE_THETA ** (-jnp.arange(0, HEAD_DIM, 2, dtype=jnp.float32) / HEAD_DIM)
    ang = positions.astype(jnp.float32)[..., None] * inv_freq
    return jnp.cos(ang)[:, None], jnp.sin(ang)[:, None]


def apply_rope(t, cos, sin):
    tf = t.astype(jnp.float32)
    t1, t2 = tf[..., : HEAD_DIM // 2], tf[..., HEAD_DIM // 2:]
    out = jnp.concatenate([t1 * cos - t2 * sin, t2 * cos + t1 * sin], axis=-1)
    return out.astype(t.dtype)


def to_heads(t, n):
    b, s, _ = t.shape
    return t.reshape(b, s, n, HEAD_DIM).transpose(0, 2, 1, 3)


def merge_heads(t):
    b, h, s, d = t.shape
    return t.transpose(0, 2, 1, 3).reshape(b, s, h * d)


def pad_axis(t, axis, before, after):
    widths = [(0, 0)] * t.ndim
    widths[axis % t.ndim] = (before, after)
    return jnp.pad(t, widths)


def three_blocks(t, nb, blk):
    t = t.reshape(t.shape[:-2] + (nb + 2, blk, t.shape[-1]))
    return jnp.concatenate([t[..., :nb, :, :], t[..., 1:nb + 1, :, :], t[..., 2:, :, :]], axis=-2)


def banded_attention(q, k, v, key_valid, half, sink=None):
    blk = half
    L = q.shape[-2]
    nb = -(-L // blk)
    extra = nb * blk - L
    qb = pad_axis(q, -2, 0, extra)
    qb = qb.reshape(qb.shape[:-2] + (nb, blk, qb.shape[-1]))
    kb = three_blocks(pad_axis(k, -2, blk, extra + blk), nb, blk)
    vb = three_blocks(pad_axis(v, -2, blk, extra + blk), nb, blk)
    valid = three_blocks(pad_axis(key_valid, -1, blk, extra + blk)[..., None], nb, blk)[..., 0]
    a = np.arange(blk)[:, None]
    c = np.arange(3 * blk)[None, :]
    band = jnp.asarray(np.abs(c - blk - a) <= half)
    mask = band & valid[..., None, :]
    scale = 1.0 / np.sqrt(HEAD_DIM)
    s = jnp.matmul(qb, jnp.swapaxes(kb, -1, -2), preferred_element_type=jnp.float32) * scale
    s = jnp.where(mask, s, NEG_INF)
    if sink is not None:
        s = jnp.concatenate([s, jnp.broadcast_to(sink.astype(jnp.float32), s.shape[:-1] + (1,))], axis=-1)
    lse = jax.nn.logsumexp(s, axis=-1, keepdims=True)
    p = jnp.exp(s - lse)
    if sink is not None:
        p = p[..., :-1]
    o = jnp.matmul(p.astype(v.dtype), vb, preferred_element_type=jnp.float32)
    o = o.reshape(o.shape[:-3] + (nb * blk, o.shape[-1]))[..., :L, :]
    lse = lse.reshape(lse.shape[:-3] + (nb * blk,))[..., :L]
    return o, lse


def dilated_mixture_attention(q, k, v):
    b, h, s, hd = q.shape
    outs, lses = [], []
    for (w, d) in DILATED_GROUPS:
        half = w // (2 * d)
        L = -(-s // d)
        sp = L * d

        def to_strided(t):
            t = pad_axis(t, 2, 0, sp - s)
            return jnp.swapaxes(t.reshape(b, h, L, d, hd), 2, 3)

        key_valid = jnp.asarray((np.arange(L)[None, :] * d + np.arange(d)[:, None]) < s)
        o, lse = banded_attention(to_strided(q), to_strided(k), to_strided(v), key_valid, half)
        outs.append(jnp.swapaxes(o, 2, 3).reshape(b, h, sp, hd)[:, :, :s])
        lses.append(jnp.swapaxes(lse, 2, 3).reshape(b, h, sp)[:, :, :s])
    weights = jax.nn.softmax(jnp.stack(lses), axis=0)
    return jnp.sum(weights[..., None] * jnp.stack(outs), axis=0)


def memory_cross_attention(h, m, w_q, w_kv, g_q, g_k, w_o):
    q = rmsnorm(to_heads(h @ w_q, N_HEADS_MEM), g_q)
    kv = m @ w_kv
    k = rmsnorm(to_heads(kv[..., :WIDTH_MEM], N_HEADS_MEM), g_k)
    v = to_heads(kv[..., WIDTH_MEM:], N_HEADS_MEM)
    s = jnp.einsum('bhqd,bhkd->bhqk', q, k, preferred_element_type=jnp.float32) / np.sqrt(HEAD_DIM)
    p = jax.nn.softmax(s, axis=-1)
    o = jnp.einsum('bhqk,bhkd->bhqd', p.astype(v.dtype), v)
    return merge_heads(o) @ w_o


def expert_choice_moe(h, w_router, w_gate, w_up, w_down):
    b, s, _ = h.shape
    cap = EC_CAPACITY * s // N_EXPERTS
    aff = jax.nn.softmax((h @ w_router).astype(jnp.float32), axis=-1)
    gate, idx = lax.top_k(jnp.swapaxes(aff, 1, 2), cap)
    b_idx = jnp.arange(b)[:, None, None]
    xe = h[b_idx, idx]
    g = jnp.einsum('becd,edf->becf', xe, w_gate)
    u = jnp.einsum('becd,edf->becf', xe, w_up)
    y = jnp.einsum('becf,efd->becd', jax.nn.silu(g) * u, w_down)
    y = y * gate[..., None].astype(y.dtype)
    return jnp.zeros_like(h).at[b_idx, idx].add(y)


def setup_inputs(seed: int = 0) -> dict:
    key = jax.random.key(seed)
    ks = jax.random.split(key, 26)
    f32 = jnp.float32

    def nrm(k, shape, scale):
        return jax.random.normal(k, shape, f32) * scale

    def gain(k, shape):
        return 1.0 + 0.02 * jax.random.normal(k, shape, f32)

    positions = jnp.broadcast_to(jnp.arange(SEQ, dtype=jnp.int32)[None], (BATCH, SEQ))
    return {
        "x": nrm(ks[0], (BATCH, SEQ, D_MODEL), 1.0),
        "mem": nrm(ks[1], (BATCH, N_MEM, D_MODEL), 1.0),
        "positions": positions,
        "g_mix": gain(ks[2], (DEPTH, D_MODEL)),
        "w_in": nrm(ks[3], (DEPTH, D_MODEL, D_IN), D_MODEL ** -0.5),
        "g_qa": gain(ks[4], (DEPTH, HEAD_DIM)),
        "g_ka": gain(ks[5], (DEPTH, HEAD_DIM)),
        "g_qb": gain(ks[6], (DEPTH, HEAD_DIM)),
        "g_kb": gain(ks[7], (DEPTH, HEAD_DIM)),
        "sink_b": nrm(ks[8], (DEPTH, N_HEADS_B), 0.5),
        "g_oa": gain(ks[9], (DEPTH, WIDTH_A)),
        "g_ob": gain(ks[10], (DEPTH, WIDTH_B)),
        "w_out": nrm(ks[11], (DEPTH, MIX_WIDTH, D_MODEL), MIX_WIDTH ** -0.5),
        "g_cross": gain(ks[12], (DEPTH, D_MODEL)),
        "g_mem": gain(ks[13], (DEPTH, D_MODEL)),
        "w_q_mem": nrm(ks[14], (DEPTH, D_MODEL, WIDTH_MEM), D_MODEL ** -0.5),
        "w_kv_mem": nrm(ks[15], (DEPTH, D_MODEL, 2 * WIDTH_MEM), D_MODEL ** -0.5),
        "g_qm": gain(ks[16], (DEPTH, HEAD_DIM)),
        "g_km": gain(ks[17], (DEPTH, HEAD_DIM)),
        "w_o_mem": nrm(ks[18], (DEPTH, WIDTH_MEM, D_MODEL), WIDTH_MEM ** -0.5),
        "g_moe": gain(ks[19], (DEPTH, D_MODEL)),
        "w_router": nrm(ks[20], (DEPTH, D_MODEL, N_EXPERTS), D_MODEL ** -0.5),
        "w_gate": nrm(ks[21], (DEPTH, N_EXPERTS, D_MODEL, D_FF_EXPERT), D_MODEL ** -0.5),
        "w_up": nrm(ks[22], (DEPTH, N_EXPERTS, D_MODEL, D_FF_EXPERT), D_MODEL ** -0.5),
        "w_down": nrm(ks[23], (DEPTH, N_EXPERTS, D_FF_EXPERT, D_MODEL), D_FF_EXPERT ** -0.5),
    }


def reference(x, mem, positions, g_mix, w_in, g_qa, g_ka, g_qb, g_kb, sink_b, g_oa, g_ob,
              w_out, g_cross, g_mem, w_q_mem, w_kv_mem, g_qm, g_km, w_o_mem, g_moe,
              w_router, w_gate, w_up, w_down):
    b, s, _ = x.shape
    cos, sin = rope_tables(positions)
    key_valid_b = jnp.ones((s,), dtype=bool)
    for l in range(DEPTH):
        h = rmsnorm(x, g_mix[l])
        proj = h @ w_in[l]
        qa, ka, va, qb, kb, vb = jnp.split(proj, IN_OFFSETS, axis=-1)
        qa = apply_rope(rmsnorm(to_heads(qa, N_HEADS_A), g_qa[l]), cos, sin)
        ka = apply_rope(rmsnorm(to_heads(ka, N_HEADS_A), g_ka[l]), cos, sin)
        va = to_heads(va, N_HEADS_A)
        oa = dilated_mixture_attention(qa, ka, va).astype(x.dtype)

        qb = apply_rope(rmsnorm(to_heads(qb, N_HEADS_B), g_qb[l]), cos, sin)
        kb = apply_rope(rmsnorm(to_heads(kb, N_KV_B), g_kb[l]), cos, sin)
        vb = to_heads(vb, N_KV_B)
        qb = qb.reshape(b, N_KV_B, GQA_REP, s, HEAD_DIM)
        sink = sink_b[l].reshape(N_KV_B, GQA_REP, 1, 1, 1)
        ob, _ = banded_attention(qb, kb[:, :, None], vb[:, :, None], key_valid_b,
                                 HALF_WINDOW_B, sink=sink)
        ob = ob.reshape(b, N_HEADS_B, s, HEAD_DIM).astype(x.dtype)

        mixed = jnp.concatenate([rmsnorm(merge_heads(oa), g_oa[l]),
                                 rmsnorm(merge_heads(ob), g_ob[l])], axis=-1)
        x = x + mixed @ w_out[l]

        x = x + memory_cross_attention(rmsnorm(x, g_cross[l]), rmsnorm(mem, g_mem[l]),
                                       w_q_mem[l], w_kv_mem[l], g_qm[l], g_km[l], w_o_mem[l])

        x = x + expert_choice_moe(rmsnorm(x, g_moe[l]), w_router[l], w_gate[l], w_up[l], w_down[l])
    return x
```

```python
import functools

import jax
import jax.numpy as jnp
from jax import lax
from jax.experimental import pallas as pl
from jax.experimental.pallas import tpu as pltpu

F32 = jnp.float32
BF16 = jnp.bfloat16

HEAD_DIM = 128
LANES = 128
EPS = 1e-6
NEG_INF = -1e30
ROPE_THETA = 10000.0
DILATED_GROUPS = ((128, 1), (512, 4), (2048, 16))
HALF_WINDOW_B = 128
GQA_REP = 4
N_HEADS_A = 8
N_HEADS_B = 8
N_HEADS_MEM = 4
N_EXPERTS = 16
EC_CAPACITY = 2
VMEM_LIMIT = 56 * 1024 * 1024


def _params(sem):
    return pltpu.CompilerParams(dimension_semantics=sem, vmem_limit_bytes=VMEM_LIMIT)


def _rms(x, g):
    return x * lax.rsqrt(jnp.mean(x * x, axis=-1, keepdims=True) + EPS) * g


def _dot(a, b):
    return jnp.dot(a, b, preferred_element_type=F32)


def _dot_nt(a, b):
    return lax.dot_general(a, b, (((1,), (1,)), ((), ())), preferred_element_type=F32)


def _rope_kernel(pos_ref, invf_ref, cos_ref, sin_ref):
    ang = pos_ref[...] * invf_ref[...]
    lane = lax.broadcasted_iota(jnp.int32, ang.shape, 1)
    cos_ref[...] = jnp.cos(ang)
    sin_ref[...] = jnp.where(lane < HEAD_DIM // 2, -jnp.sin(ang), jnp.sin(ang))


def _rope_tables(positions):
    b, s = positions.shape
    t = b * s
    inv_freq = ROPE_THETA ** (-jnp.arange(0, HEAD_DIM, 2, dtype=F32) / HEAD_DIM)
    invf2 = jnp.concatenate([inv_freq, inv_freq]).reshape(1, HEAD_DIM)
    pos = jnp.broadcast_to(positions.reshape(t, 1).astype(F32), (t, HEAD_DIM))
    ts = 1024
    cos2, sin2 = pl.pallas_call(
        _rope_kernel,
        grid=(t // ts,),
        in_specs=[pl.BlockSpec((ts, HEAD_DIM), lambda i: (i, 0)),
                  pl.BlockSpec((1, HEAD_DIM), lambda i: (0, 0))],
        out_specs=[pl.BlockSpec((ts, HEAD_DIM), lambda i: (i, 0))] * 2,
        out_shape=[jax.ShapeDtypeStruct((t, HEAD_DIM), F32)] * 2,
        compiler_params=_params(("arbitrary",)),
        name="rope_tables",
    )(pos, invf2)
    return cos2.reshape(b, s, HEAD_DIM), sin2.reshape(b, s, HEAD_DIM)


ROW_CHUNK = 256


def _norm_matmul_kernel(x_ref, g_ref, w_ref, o_ref, xn_ref):
    @pl.when(pl.program_id(1) == 0)
    def _():
        def body(c, _):
            rows = pl.ds(c * ROW_CHUNK, ROW_CHUNK)
            xn_ref[rows, :] = _rms(x_ref[rows, :], g_ref[...]).astype(BF16)
            return 0
        lax.fori_loop(0, x_ref.shape[0] // ROW_CHUNK, body, 0)

    o_ref[...] = _dot(xn_ref[...], w_ref[...].astype(BF16))


def _norm_matmul(x, g, w, tm, tn):
    t, k = x.shape
    n = w.shape[1]
    return pl.pallas_call(
        _norm_matmul_kernel,
        grid=(t // tm, n // tn),
        in_specs=[pl.BlockSpec((tm, k), lambda i, j: (i, 0)),
                  pl.BlockSpec((1, k), lambda i, j: (0, 0)),
                  pl.BlockSpec((k, tn), lambda i, j: (0, j))],
        out_specs=pl.BlockSpec((tm, tn), lambda i, j: (i, j)),
        out_shape=jax.ShapeDtypeStruct((t, n), F32),
        scratch_shapes=[pltpu.VMEM((tm, k), BF16)],
        compiler_params=_params(("arbitrary", "arbitrary")),
        name="norm_in_proj",
    )(x, g.reshape(1, k), w)


def _norm_rope_into(src_ref, g_ref, cos_ref, sin_ref, dst_ref, scale):
    def body(c, _):
        rows = pl.ds(c * ROW_CHUNK, ROW_CHUNK)
        t = _rms(src_ref[0, rows, :], g_ref[...])
        t = t * cos_ref[0, rows, :] + pltpu.roll(t, HEAD_DIM // 2, axis=1) * sin_ref[0, rows, :]
        dst_ref[rows, :] = t * scale
        return 0
    lax.fori_loop(0, dst_ref.shape[0] // ROW_CHUNK, body, 0)


def _band_mask(q0, k0, nq, nk, half):
    qpos = q0 + lax.broadcasted_iota(jnp.int32, (nq, nk), 0)
    kpos = k0 + lax.broadcasted_iota(jnp.int32, (nq, nk), 1)
    return jnp.abs(qpos - kpos) <= half


QBLK_A = 128
KWIN_A = 256


def _rows(start, size, stride):
    return pl.ds(start, size) if stride == 1 else pl.ds(start, size, stride=stride)


def _attn_a_kernel(q_ref, k_ref, v_ref, cos_ref, sin_ref, gq_ref, gk_ref, o_ref,
                   qn_ref, kn_ref, acc_ref, m_ref, l_ref):
    s = qn_ref.shape[0]
    _norm_rope_into(q_ref, gq_ref, cos_ref, sin_ref, qn_ref, 1.0 / (HEAD_DIM ** 0.5))
    _norm_rope_into(k_ref, gk_ref, cos_ref, sin_ref, kn_ref, 1.0)

    for gi, (w, d) in enumerate(DILATED_GROUPS):
        half = w // (2 * d)
        ln = s // d
        nblk = ln // QBLK_A

        def body(it, _, d=d, half=half, ln=ln, nblk=nblk, first=(gi == 0)):
            r = it // nblk
            a = (it % nblk) * QBLK_A
            k0 = jnp.clip(a - (KWIN_A - QBLK_A) // 2, 0, ln - KWIN_A)
            qrows = _rows(r + d * a, QBLK_A, d)
            krows = _rows(r + d * k0, KWIN_A, d)
            q = qn_ref[qrows, :].astype(BF16)
            k = kn_ref[krows, :].astype(BF16)
            v = v_ref[0, krows, :].astype(BF16)
            sc = _dot_nt(q, k)
            sc = jnp.where(_band_mask(a, k0, QBLK_A, KWIN_A, half), sc, NEG_INF)
            m_blk = jnp.max(sc, axis=-1, keepdims=True)
            if first:
                p = jnp.exp(sc - m_blk)
                m_ref[qrows, :] = m_blk
                l_ref[qrows, :] = jnp.sum(p, axis=-1, keepdims=True)
                acc_ref[qrows, :] = _dot(p.astype(BF16), v)
            else:
                m_old = m_ref[qrows, :]
                m_new = jnp.maximum(m_old, m_blk)
                alpha = jnp.exp(m_old - m_new)
                p = jnp.exp(sc - m_new)
                m_ref[qrows, :] = m_new
                l_ref[qrows, :] = alpha * l_ref[qrows, :] + jnp.sum(p, axis=-1, keepdims=True)
                acc_ref[qrows, :] = alpha * acc_ref[qrows, :] + _dot(p.astype(BF16), v)
            return 0

        lax.fori_loop(0, d * nblk, body, 0)

    def fin(c, _):
        rows = pl.ds(c * ROW_CHUNK, ROW_CHUNK)
        o_ref[0, rows, :] = acc_ref[rows, :] / l_ref[rows, :]
        return 0
    lax.fori_loop(0, s // ROW_CHUNK, fin, 0)


def _attn_a(proj3, cos2, sin2, g_q, g_k, col_q, col_k, col_v):
    b, s, _ = proj3.shape
    nh = N_HEADS_A
    blk = lambda col: pl.BlockSpec((1, s, HEAD_DIM), lambda i, h, col=col: (i, 0, col + h))
    tab = pl.BlockSpec((1, s, HEAD_DIM), lambda i, h: (i, 0, 0))
    gain = pl.BlockSpec((1, HEAD_DIM), lambda i, h: (0, 0))
    return pl.pallas_call(
        _attn_a_kernel,
        grid=(b, nh),
        in_specs=[blk(col_q), blk(col_k), blk(col_v), tab, tab, gain, gain],
        out_specs=pl.BlockSpec((1, s, HEAD_DIM), lambda i, h: (i, 0, h)),
        out_shape=jax.ShapeDtypeStruct((b, s, nh * HEAD_DIM), F32),
        scratch_shapes=[pltpu.VMEM((s, HEAD_DIM), F32), pltpu.VMEM((s, HEAD_DIM), F32),
                        pltpu.VMEM((s, HEAD_DIM), F32), pltpu.VMEM((s, 1), F32),
                        pltpu.VMEM((s, 1), F32)],
        compiler_params=_params(("arbitrary", "arbitrary")),
        name="attn_dilated",
    )(proj3, proj3, proj3, cos2, sin2, g_q.reshape(1, HEAD_DIM), g_k.reshape(1, HEAD_DIM))


QBLK_B = 256
KWIN_B = QBLK_B + 2 * HALF_WINDOW_B


def _attn_b_kernel(sink_ref, q_ref, k_ref, v_ref, cos_ref, sin_ref, gq_ref, gk_ref, o_ref,
                   qn_ref, kn_ref):
    s = qn_ref.shape[0]
    sink = sink_ref[pl.program_id(1)]
    _norm_rope_into(q_ref, gq_ref, cos_ref, sin_ref, qn_ref, 1.0 / (HEAD_DIM ** 0.5))
    _norm_rope_into(k_ref, gk_ref, cos_ref, sin_ref, kn_ref, 1.0)

    def body(i, _):
        a = i * QBLK_B
        k0 = jnp.clip(a - HALF_WINDOW_B, 0, s - KWIN_B)
        q = qn_ref[pl.ds(a, QBLK_B), :].astype(BF16)
        k = kn_ref[pl.ds(k0, KWIN_B), :].astype(BF16)
        v = v_ref[0, pl.ds(k0, KWIN_B), :].astype(BF16)
        sc = _dot_nt(q, k)
        sc = jnp.where(_band_mask(a, k0, QBLK_B, KWIN_B, HALF_WINDOW_B), sc, NEG_INF)
        m = jnp.maximum(jnp.max(sc, axis=-1, keepdims=True), sink)
        p = jnp.exp(sc - m)
        l = jnp.sum(p, axis=-1, keepdims=True) + jnp.exp(sink - m)
        o_ref[0, pl.ds(a, QBLK_B), :] = _dot(p.astype(BF16), v) / l
        return 0
    lax.fori_loop(0, s // QBLK_B, body, 0)


def _attn_b(proj3, cos2, sin2, g_q, g_k, sink, col_q, col_k, col_v):
    b, s, _ = proj3.shape
    nh = N_HEADS_B
    qblk = pl.BlockSpec((1, s, HEAD_DIM), lambda i, h: (i, 0, col_q + h))
    kblk = pl.BlockSpec((1, s, HEAD_DIM), lambda i, h: (i, 0, col_k + h // GQA_REP))
    vblk = pl.BlockSpec((1, s, HEAD_DIM), lambda i, h: (i, 0, col_v + h // GQA_REP))
    tab = pl.BlockSpec((1, s, HEAD_DIM), lambda i, h: (i, 0, 0))
    gain = pl.BlockSpec((1, HEAD_DIM), lambda i, h: (0, 0))
    return pl.pallas_call(
        _attn_b_kernel,
        grid=(b, nh),
        in_specs=[pl.BlockSpec(memory_space=pltpu.SMEM), qblk, kblk, vblk, tab, tab, gain, gain],
        out_specs=pl.BlockSpec((1, s, HEAD_DIM), lambda i, h: (i, 0, h)),
        out_shape=jax.ShapeDtypeStruct((b, s, nh * HEAD_DIM), F32),
        scratch_shapes=[pltpu.VMEM((s, HEAD_DIM), F32), pltpu.VMEM((s, HEAD_DIM), F32)],
        compiler_params=_params(("arbitrary", "arbitrary")),
        name="attn_window_gqa",
    )(sink, proj3, proj3, proj3, cos2, sin2, g_q.reshape(1, HEAD_DIM), g_k.reshape(1, HEAD_DIM))


def _out_proj_kernel(oa_ref, ob_ref, ga_ref, gb_ref, w_ref, x_ref, o_ref, mix_ref):
    wa = oa_ref.shape[1]

    @pl.when(pl.program_id(1) == 0)
    def _():
        def body(c, _):
            rows = pl.ds(c * ROW_CHUNK, ROW_CHUNK)
            mix_ref[rows, :wa] = _rms(oa_ref[rows, :], ga_ref[...]).astype(BF16)
            mix_ref[rows, wa:] = _rms(ob_ref[rows, :], gb_ref[...]).astype(BF16)
            return 0
        lax.fori_loop(0, oa_ref.shape[0] // ROW_CHUNK, body, 0)

    o_ref[...] = x_ref[...] + _dot(mix_ref[...], w_ref[...].astype(BF16))


def _out_proj(oa, ob, g_oa, g_ob, w_out, x, tm, tn):
    t, wa = oa.shape
    wb = ob.shape[1]
    k, n = w_out.shape
    return pl.pallas_call(
        _out_proj_kernel,
        grid=(t // tm, n // tn),
        in_specs=[pl.BlockSpec((tm, wa), lambda i, j: (i, 0)),
                  pl.BlockSpec((tm, wb), lambda i, j: (i, 0)),
                  pl.BlockSpec((1, wa), lambda i, j: (0, 0)),
                  pl.BlockSpec((1, wb), lambda i, j: (0, 0)),
                  pl.BlockSpec((k, tn), lambda i, j: (0, j)),
                  pl.BlockSpec((tm, tn), lambda i, j: (i, j))],
        out_specs=pl.BlockSpec((tm, tn), lambda i, j: (i, j)),
        out_shape=jax.ShapeDtypeStruct((t, n), F32),
        scratch_shapes=[pltpu.VMEM((tm, k), BF16)],
        compiler_params=_params(("arbitrary", "arbitrary")),
        name="mix_out_proj",
    )(oa, ob, g_oa.reshape(1, wa), g_ob.reshape(1, wb), w_out, x)


def _mem_kv_kernel(m_ref, g_ref, w_ref, gk_ref, k_ref, v_ref):
    wm = k_ref.shape[2]
    mn = _rms(m_ref[0], g_ref[...]).astype(BF16)
    kv = _dot(mn, w_ref[...].astype(BF16))
    for h in range(wm // HEAD_DIM):
        cols = slice(h * HEAD_DIM, (h + 1) * HEAD_DIM)
        k_ref[0, :, cols] = _rms(kv[:, cols], gk_ref[...]).astype(BF16)
    v_ref[0] = kv[:, wm:].astype(BF16)


def _mem_kv(mem, g_mem, w_kv, g_km):
    b, nm, d = mem.shape
    wm = w_kv.shape[1] // 2
    return pl.pallas_call(
        _mem_kv_kernel,
        grid=(b,),
        in_specs=[pl.BlockSpec((1, nm, d), lambda i: (i, 0, 0)),
                  pl.BlockSpec((1, d), lambda i: (0, 0)),
                  pl.BlockSpec((d, 2 * wm), lambda i: (0, 0)),
                  pl.BlockSpec((1, HEAD_DIM), lambda i: (0, 0))],
        out_specs=[pl.BlockSpec((1, nm, wm), lambda i: (i, 0, 0))] * 2,
        out_shape=[jax.ShapeDtypeStruct((b, nm, wm), BF16)] * 2,
        compiler_params=_params(("arbitrary",)),
        name="mem_kv_proj",
    )(mem, g_mem.reshape(1, d), w_kv, g_km.reshape(1, HEAD_DIM))


def _cross_kernel(x_ref, g_ref, wq_ref, gq_ref, k_ref, v_ref, wo_ref, o_ref, wq16_ref, wo16_ref):
    @pl.when(pl.program_id(0) == 0)
    def _():
        wq16_ref[...] = wq_ref[...].astype(BF16)
        wo16_ref[...] = wo_ref[...].astype(BF16)

    x = x_ref[...]
    hn = _rms(x, g_ref[...]).astype(BF16)
    q = _dot(hn, wq16_ref[...])
    outs = []
    for h in range(q.shape[1] // HEAD_DIM):
        cols = slice(h * HEAD_DIM, (h + 1) * HEAD_DIM)
        qh = (_rms(q[:, cols], gq_ref[...]) * (1.0 / (HEAD_DIM ** 0.5))).astype(BF16)
        sc = _dot_nt(qh, k_ref[0, :, cols])
        m = jnp.max(sc, axis=-1, keepdims=True)
        p = jnp.exp(sc - m)
        l = jnp.sum(p, axis=-1, keepdims=True)
        outs.append((_dot(p.astype(BF16), v_ref[0, :, cols]) / l).astype(BF16))
    o = jnp.concatenate(outs, axis=-1)
    o_ref[...] = x + _dot(o, wo16_ref[...])


def _cross_attn(x, g_cross, w_q, g_qm, kmem, vmem, w_o, s, tm):
    t, d = x.shape
    wm = w_q.shape[1]
    nm = kmem.shape[1]
    per_b = s // tm
    return pl.pallas_call(
        _cross_kernel,
        grid=(t // tm,),
        in_specs=[pl.BlockSpec((tm, d), lambda i: (i, 0)),
                  pl.BlockSpec((1, d), lambda i: (0, 0)),
                  pl.BlockSpec((d, wm), lambda i: (0, 0)),
                  pl.BlockSpec((1, HEAD_DIM), lambda i: (0, 0)),
                  pl.BlockSpec((1, nm, wm), lambda i: (i // per_b, 0, 0)),
                  pl.BlockSpec((1, nm, wm), lambda i: (i // per_b, 0, 0)),
                  pl.BlockSpec((wm, d), lambda i: (0, 0))],
        out_specs=pl.BlockSpec((tm, d), lambda i: (i, 0)),
        out_shape=jax.ShapeDtypeStruct((t, d), F32),
        scratch_shapes=[pltpu.VMEM((d, wm), BF16), pltpu.VMEM((wm, d), BF16)],
        compiler_params=_params(("arbitrary",)),
        name="mem_cross_attn",
    )(x, g_cross.reshape(1, d), w_q, g_qm.reshape(1, HEAD_DIM), kmem, vmem, w_o)


def _router_kernel(x_ref, g_ref, wr_ref, h_ref, aff_ref, *, n_experts):
    h = _rms(x_ref[...], g_ref[...])
    h_ref[...] = h
    logits = jnp.dot(h, wr_ref[...], preferred_element_type=F32, precision=lax.Precision.HIGHEST)
    lane = lax.broadcasted_iota(jnp.int32, logits.shape, 1)
    logits = jnp.where(lane < n_experts, logits, NEG_INF)
    m = jnp.max(logits, axis=-1, keepdims=True)
    e = jnp.exp(logits - m)
    aff_ref[...] = e / jnp.sum(e, axis=-1, keepdims=True)


def _router(x, g_moe, w_router, tm):
    t, d = x.shape
    ne = w_router.shape[1]
    wr = jnp.pad(w_router, ((0, 0), (0, LANES - ne)))
    return pl.pallas_call(
        functools.partial(_router_kernel, n_experts=ne),
        grid=(t // tm,),
        in_specs=[pl.BlockSpec((tm, d), lambda i: (i, 0)),
                  pl.BlockSpec((1, d), lambda i: (0, 0)),
                  pl.BlockSpec((d, LANES), lambda i: (0, 0))],
        out_specs=[pl.BlockSpec((tm, d), lambda i: (i, 0)),
                   pl.BlockSpec((tm, LANES), lambda i: (i, 0))],
        out_shape=[jax.ShapeDtypeStruct((t, d), F32), jax.ShapeDtypeStruct((t, LANES), F32)],
        compiler_params=_params(("arbitrary",)),
        name="moe_router",
    )(x, g_moe.reshape(1, d), wr)


PREFIX_BLK = 256
SEL_CHUNK = 256


def _prefix_rows_into(mask_ref, out_ref, ltri):
    carry = jnp.zeros((1, mask_ref.shape[1]), F32)
    for blk in range(mask_ref.shape[0] // PREFIX_BLK):
        rows = pl.ds(blk * PREFIX_BLK, PREFIX_BLK)
        mb = mask_ref[rows, :]
        out_ref[rows, :] = _dot(ltri, mb.astype(BF16)) + carry
        carry = carry + jnp.sum(mb, axis=0, keepdims=True)


def _select_kernel(aff_ref, idx_ref, gate_ref, bits_ref, mask_ref, pre_ref, key_ref,
                   *, n_experts, cap):
    s = aff_ref.shape[1]
    bits_ref[...] = pltpu.bitcast(aff_ref[0], jnp.int32)

    def search(i, tau):
        cand = tau | jnp.left_shift(jnp.int32(1), 30 - i)
        cnt = jnp.sum((bits_ref[...] >= cand).astype(F32), axis=0, keepdims=True)
        return jnp.where(cnt >= cap, cand, tau)
    tau = lax.fori_loop(0, 31, search, jnp.zeros((1, LANES), jnp.int32))

    r = lax.broadcasted_iota(jnp.int32, (PREFIX_BLK, PREFIX_BLK), 0)
    c = lax.broadcasted_iota(jnp.int32, (PREFIX_BLK, PREFIX_BLK), 1)
    ltri = (c < r).astype(BF16)

    gt = bits_ref[...] > tau
    n_gt = jnp.sum(gt.astype(F32), axis=0, keepdims=True)
    mask_ref[...] = (bits_ref[...] == tau).astype(F32)
    _prefix_rows_into(mask_ref, pre_ref, ltri)
    sel = gt | ((mask_ref[...] > 0.5) & (pre_ref[...] < cap - n_gt))
    mask_ref[...] = sel.astype(F32)
    _prefix_rows_into(mask_ref, pre_ref, ltri)
    key_ref[...] = jnp.where(mask_ref[...] > 0.5, pre_ref[...], -1.0)

    slot_iota = lax.broadcasted_iota(jnp.int32, (SEL_CHUNK, cap), 1).astype(F32)
    row_iota = lax.broadcasted_iota(jnp.int32, (SEL_CHUNK, cap), 0).astype(F32)
    base = pl.program_id(0) * s
    for e in range(n_experts):
        def body(ch, carry, e=e):
            idx_acc, gate_acc = carry
            rows = pl.ds(ch * SEL_CHUNK, SEL_CHUNK)
            hit = key_ref[rows, e:e + 1] == slot_iota
            tok = row_iota + jnp.asarray(base + ch * SEL_CHUNK, F32)
            idx_c = jnp.where(hit, tok, 0.0)
            gate_c = jnp.where(hit, aff_ref[0, rows, e:e + 1], 0.0)
            idx_acc = idx_acc + idx_c.reshape(SEL_CHUNK // 8, 8, cap).sum(axis=0)
            gate_acc = gate_acc + gate_c.reshape(SEL_CHUNK // 8, 8, cap).sum(axis=0)
            return idx_acc, gate_acc
        zero = jnp.zeros((8, cap), F32)
        idx_acc, gate_acc = lax.fori_loop(0, s // SEL_CHUNK, body, (zero, zero))
        idx_ref[0, e:e + 1, :] = jnp.sum(idx_acc, axis=0, keepdims=True).astype(jnp.int32)
        gate_ref[0, e:e + 1, :] = jnp.sum(gate_acc, axis=0, keepdims=True)


def _select(aff3, n_experts, cap):
    b, s, _ = aff3.shape
    return pl.pallas_call(
        functools.partial(_select_kernel, n_experts=n_experts, cap=cap),
        grid=(b,),
        in_specs=[pl.BlockSpec((1, s, LANES), lambda i: (i, 0, 0))],
        out_specs=[pl.BlockSpec((1, n_experts, cap), lambda i: (i, 0, 0))] * 2,
        out_shape=[jax.ShapeDtypeStruct((b, n_experts, cap), jnp.int32),
                   jax.ShapeDtypeStruct((b, n_experts, cap), F32)],
        scratch_shapes=[pltpu.VMEM((s, LANES), jnp.int32), pltpu.VMEM((s, LANES), F32),
                        pltpu.VMEM((s, LANES), F32), pltpu.VMEM((s, LANES), F32)],
        compiler_params=_params(("arbitrary",)),
        name="moe_select",
    )(aff3)


def _moe_kernel(idx_ref, h_hbm, x_hbm, gate_ref, wg_ref, wu_ref, wd_ref, o_hbm,
                xe32_ref, xe16_ref, y_ref, row_ref, sem_ref, *, rows, n_experts, n_f):
    del x_hbm
    e = pl.program_id(0)
    f = pl.program_id(1)
    gather_sem, acc_sem, scatter_sem = sem_ref.at[0], sem_ref.at[1], sem_ref.at[2]

    def row_copies(ex, src_hbm, dst_hbm, vmem_ref, sem, to_vmem):
        def body(i, _):
            t = idx_ref[ex * rows + i]
            if to_vmem:
                pltpu.make_async_copy(src_hbm.at[pl.ds(t, 1)], vmem_ref.at[pl.ds(i, 1)], sem).start()
            else:
                pltpu.make_async_copy(vmem_ref.at[pl.ds(i, 1)], dst_hbm.at[pl.ds(t, 1)], sem).start()
            return 0
        lax.fori_loop(0, rows, body, 0)

    def wait_rows(hbm, vmem_ref, sem):
        pltpu.make_async_copy(hbm.at[pl.ds(0, rows)], vmem_ref, sem).wait()

    @pl.when((e == 0) & (f == 0))
    def _():
        row_copies(0, h_hbm, None, xe32_ref, gather_sem, True)

    @pl.when(f == 0)
    def _():
        wait_rows(h_hbm, xe32_ref, gather_sem)
        xe16_ref[...] = xe32_ref[...].astype(BF16)

        @pl.when(e + 1 < n_experts)
        def _():
            row_copies(e + 1, h_hbm, None, xe32_ref, gather_sem, True)

    @pl.when(f == 1)
    def _():
        @pl.when(e > 0)
        def _():
            wait_rows(o_hbm, row_ref, scatter_sem)
        row_copies(e, o_hbm, None, row_ref, acc_sem, True)

    xe = xe16_ref[...]
    g = _dot(xe, wg_ref[0].astype(BF16))
    u = _dot(xe, wu_ref[0].astype(BF16))
    act = (g * (1.0 / (1.0 + jnp.exp(-g))) * u).astype(BF16)
    contrib = _dot(act, wd_ref[0].astype(BF16))

    @pl.when(f == 0)
    def _():
        y_ref[...] = contrib

    @pl.when(f > 0)
    def _():
        y_ref[...] += contrib

    @pl.when(f == n_f - 1)
    def _():
        wait_rows(o_hbm, row_ref, acc_sem)
        row_ref[...] += y_ref[...] * gate_ref[0, :, 0:1]
        row_copies(e, None, o_hbm, row_ref, scatter_sem, False)

        @pl.when(e == n_experts - 1)
        def _():
            wait_rows(o_hbm, row_ref, scatter_sem)


def _moe_experts(idx_flat, gate_rows, h, x, w_gate, w_up, w_down, tf):
    t, d = x.shape
    ne, _, dff = w_gate.shape
    rows = idx_flat.shape[0] // ne
    n_f = dff // tf
    assert n_f >= 2
    grid_spec = pltpu.PrefetchScalarGridSpec(
        num_scalar_prefetch=1,
        grid=(ne, n_f),
        in_specs=[pl.BlockSpec(memory_space=pl.ANY),
                  pl.BlockSpec(memory_space=pl.ANY),
                  pl.BlockSpec((1, rows, LANES), lambda e, f, idx: (e, 0, 0)),
                  pl.BlockSpec((1, d, tf), lambda e, f, idx: (e, 0, f)),
                  pl.BlockSpec((1, d, tf), lambda e, f, idx: (e, 0, f)),
                  pl.BlockSpec((1, tf, d), lambda e, f, idx: (e, f, 0))],
        out_specs=pl.BlockSpec(memory_space=pl.ANY),
        scratch_shapes=[pltpu.VMEM((rows, d), F32), pltpu.VMEM((rows, d), BF16),
                        pltpu.VMEM((rows, d), F32), pltpu.VMEM((rows, d), F32),
                        pltpu.SemaphoreType.DMA((3,))],
    )
    return pl.pallas_call(
        functools.partial(_moe_kernel, rows=rows, n_experts=ne, n_f=n_f),
        grid_spec=grid_spec,
        out_shape=jax.ShapeDtypeStruct((t, d), F32),
        input_output_aliases={2: 0},
        compiler_params=_params(("arbitrary", "arbitrary")),
        name="moe_experts",
    )(idx_flat, h, x, gate_rows, w_gate, w_up, w_down)


def _moe(x, s, g_moe, w_router, w_gate, w_up, w_down):
    t, d = x.shape
    b = t // s
    ne = w_router.shape[1]
    cap = EC_CAPACITY * s // ne
    h, aff = _router(x, g_moe, w_router, tm=512)
    idx, gate = _select(aff.reshape(b, s, LANES), ne, cap)
    idx_flat = idx.transpose(1, 0, 2).reshape(ne * b * cap)
    gate_rows = jnp.broadcast_to(gate.transpose(1, 0, 2).reshape(ne, b * cap, 1), (ne, b * cap, LANES))
    return _moe_experts(idx_flat, gate_rows, h, x, w_gate, w_up, w_down, tf=256)


def kernel(x, mem, positions, g_mix, w_in, g_qa, g_ka, g_qb, g_kb, sink_b, g_oa, g_ob, w_out,
           g_cross, g_mem, w_q_mem, w_kv_mem, g_qm, g_km, w_o_mem, g_moe, w_router, w_gate,
           w_up, w_down):
    b, s, d = x.shape
    t = b * s
    depth = g_mix.shape[0]
    width_a = N_HEADS_A * HEAD_DIM
    width_b = N_HEADS_B * HEAD_DIM
    kv_b = width_b // GQA_REP
    col = [0, width_a, 2 * width_a, 3 * width_a, 3 * width_a + width_b, 3 * width_a + width_b + kv_b]
    col = [c // HEAD_DIM for c in col]

    cos2, sin2 = _rope_tables(positions)
    xf = x.reshape(t, d)
    for l in range(depth):
        proj = _norm_matmul(xf, g_mix[l], w_in[l], tm=1024, tn=512)
        proj3 = proj.reshape(b, s, proj.shape[1])
        oa = _attn_a(proj3, cos2, sin2, g_qa[l], g_ka[l], col[0], col[1], col[2])
        ob = _attn_b(proj3, cos2, sin2, g_qb[l], g_kb[l], sink_b[l], col[3], col[4], col[5])
        xf = _out_proj(oa.reshape(t, width_a), ob.reshape(t, width_b), g_oa[l], g_ob[l], w_out[l],
                       xf, tm=1024, tn=512)
        kmem, vmem = _mem_kv(mem, g_mem[l], w_kv_mem[l], g_km[l])
        xf = _cross_attn(xf, g_cross[l], w_q_mem[l], g_qm[l], kmem, vmem, w_o_mem[l], s, tm=512)
        xf = _moe(xf, s, g_moe[l], w_router[l], w_gate[l], w_up[l], w_down[l])
    return xf.reshape(b, s, d)
```

```python
import functools

import jax
import jax.numpy as jnp
from jax import lax
from jax.experimental import pallas as pl
from jax.experimental.pallas import tpu as pltpu

F32 = jnp.float32
BF16 = jnp.bfloat16

HEAD_DIM = 128
LANES = 128
EPS = 1e-6
NEG_INF = -1e30
ROPE_THETA = 10000.0
DILATED_GROUPS = ((128, 1), (512, 4), (2048, 16))
HALF_WINDOW_B = 128
GQA_REP = 4
N_HEADS_A = 8
N_HEADS_B = 8
N_HEADS_MEM = 4
N_EXPERTS = 16
EC_CAPACITY = 2
VMEM_LIMIT = 56 * 1024 * 1024
MOE_VMEM_LIMIT = 60 * 1024 * 1024


def _params(sem):
    return pltpu.CompilerParams(dimension_semantics=sem, vmem_limit_bytes=VMEM_LIMIT)


def _rms(x, g):
    return x * lax.rsqrt(jnp.mean(x * x, axis=-1, keepdims=True) + EPS) * g


def _dot(a, b):
    return jnp.dot(a, b, preferred_element_type=F32)


def _dot_nt(a, b):
    return lax.dot_general(a, b, (((1,), (1,)), ((), ())), preferred_element_type=F32)


def _rope_kernel(pos_ref, invf_ref, cos_ref, sin_ref):
    ang = pos_ref[...] * invf_ref[...]
    lane = lax.broadcasted_iota(jnp.int32, ang.shape, 1)
    cos_ref[...] = jnp.cos(ang)
    sin_ref[...] = jnp.where(lane < HEAD_DIM // 2, -jnp.sin(ang), jnp.sin(ang))


def _rope_tables(positions):
    b, s = positions.shape
    t = b * s
    inv_freq = ROPE_THETA ** (-jnp.arange(0, HEAD_DIM, 2, dtype=F32) / HEAD_DIM)
    invf2 = jnp.concatenate([inv_freq, inv_freq]).reshape(1, HEAD_DIM)
    pos = jnp.broadcast_to(positions.reshape(t, 1).astype(F32), (t, HEAD_DIM))
    ts = 1024
    cos2, sin2 = pl.pallas_call(
        _rope_kernel,
        grid=(t // ts,),
        in_specs=[pl.BlockSpec((ts, HEAD_DIM), lambda i: (i, 0)),
                  pl.BlockSpec((1, HEAD_DIM), lambda i: (0, 0))],
        out_specs=[pl.BlockSpec((ts, HEAD_DIM), lambda i: (i, 0))] * 2,
        out_shape=[jax.ShapeDtypeStruct((t, HEAD_DIM), F32)] * 2,
        compiler_params=_params(("arbitrary",)),
        name="rope_tables",
    )(pos, invf2)
    return cos2.reshape(b, s, HEAD_DIM), sin2.reshape(b, s, HEAD_DIM)


ROW_CHUNK = 256
ROPE_CHUNK = 1024


def _norm_matmul_kernel(x_ref, g_ref, w_ref, o_ref, xn_ref):
    @pl.when(pl.program_id(1) == 0)
    def _():
        def body(c, _):
            rows = pl.ds(c * ROW_CHUNK, ROW_CHUNK)
            xn_ref[rows, :] = _rms(x_ref[rows, :], g_ref[...]).astype(BF16)
            return 0
        lax.fori_loop(0, x_ref.shape[0] // ROW_CHUNK, body, 0)

    o_ref[...] = _dot(xn_ref[...], w_ref[...].astype(BF16))


def _norm_matmul(x, g, w, tm, tn):
    t, k = x.shape
    n = w.shape[1]
    return pl.pallas_call(
        _norm_matmul_kernel,
        grid=(t // tm, n // tn),
        in_specs=[pl.BlockSpec((tm, k), lambda i, j: (i, 0)),
                  pl.BlockSpec((1, k), lambda i, j: (0, 0)),
                  pl.BlockSpec((k, tn), lambda i, j: (0, j))],
        out_specs=pl.BlockSpec((tm, tn), lambda i, j: (i, j)),
        out_shape=jax.ShapeDtypeStruct((t, n), F32),
        scratch_shapes=[pltpu.VMEM((tm, k), BF16)],
        compiler_params=_params(("arbitrary", "arbitrary")),
        name="norm_in_proj",
    )(x, g.reshape(1, k), w)


def _norm_rope_into(src_ref, g_ref, cos_ref, sin_ref, dst_ref, scale):
    def body(c, _):
        rows = pl.ds(c * ROPE_CHUNK, ROPE_CHUNK)
        t = _rms(src_ref[0, rows, :], g_ref[...])
        t = t * cos_ref[0, rows, :] + pltpu.roll(t, HEAD_DIM // 2, axis=1) * sin_ref[0, rows, :]
        dst_ref[rows, :] = t * scale
        return 0
    lax.fori_loop(0, dst_ref.shape[0] // ROPE_CHUNK, body, 0)


def _band_mask(q0, k0, nq, nk, half):
    qpos = q0 + lax.broadcasted_iota(jnp.int32, (nq, nk), 0)
    kpos = k0 + lax.broadcasted_iota(jnp.int32, (nq, nk), 1)
    return jnp.abs(qpos - kpos) <= half


QBLK_A = 128
KWIN_A = 256
BLOCKS_PER_ITER_A = 4


def _rows(start, size, stride):
    return pl.ds(start, size) if stride == 1 else pl.ds(start, size, stride=stride)


def _attn_a_kernel(q_ref, k_ref, v_ref, cos_ref, sin_ref, gq_ref, gk_ref, o_ref,
                   qn_ref, kn_ref, acc_ref, m_ref, l_ref):
    s = qn_ref.shape[0]
    _norm_rope_into(q_ref, gq_ref, cos_ref, sin_ref, qn_ref, 1.0 / (HEAD_DIM ** 0.5))
    _norm_rope_into(k_ref, gk_ref, cos_ref, sin_ref, kn_ref, 1.0)

    for gi, (w, d) in enumerate(DILATED_GROUPS):
        half = w // (2 * d)
        ln = s // d
        nblk = ln // QBLK_A

        def body(it, _, d=d, half=half, ln=ln, nblk=nblk, first=(gi == 0)):
            blocks = []
            for j in range(BLOCKS_PER_ITER_A):
                n = it * BLOCKS_PER_ITER_A + j
                r = n // nblk
                a = (n % nblk) * QBLK_A
                k0 = jnp.clip(a - (KWIN_A - QBLK_A) // 2, 0, ln - KWIN_A)
                qrows = _rows(r + d * a, QBLK_A, d)
                krows = _rows(r + d * k0, KWIN_A, d)
                q = qn_ref[qrows, :].astype(BF16)
                k = kn_ref[krows, :].astype(BF16)
                v = v_ref[0, krows, :].astype(BF16)
                old = None if first else (m_ref[qrows, :], l_ref[qrows, :], acc_ref[qrows, :])
                blocks.append((a, k0, qrows, q, k, v, old))
            results = []
            for a, k0, qrows, q, k, v, old in blocks:
                sc = _dot_nt(q, k)
                sc = jnp.where(_band_mask(a, k0, QBLK_A, KWIN_A, half), sc, NEG_INF)
                m_new = jnp.max(sc, axis=-1, keepdims=True)
                if old is not None:
                    m_new = jnp.maximum(old[0], m_new)
                p = jnp.exp(sc - m_new)
                l_new = jnp.sum(p, axis=-1, keepdims=True)
                acc_new = _dot(p.astype(BF16), v)
                if old is not None:
                    alpha = jnp.exp(old[0] - m_new)
                    l_new = alpha * old[1] + l_new
                    acc_new = alpha * old[2] + acc_new
                results.append((qrows, m_new, l_new, acc_new))
            for qrows, m_new, l_new, acc_new in results:
                m_ref[qrows, :] = m_new
                l_ref[qrows, :] = l_new
                acc_ref[qrows, :] = acc_new
            return 0

        lax.fori_loop(0, d * nblk // BLOCKS_PER_ITER_A, body, 0)

    def fin(c, _):
        rows = pl.ds(c * ROW_CHUNK, ROW_CHUNK)
        o_ref[0, rows, :] = acc_ref[rows, :] / l_ref[rows, :]
        return 0
    lax.fori_loop(0, s // ROW_CHUNK, fin, 0)


def _attn_a(proj3, cos2, sin2, g_q, g_k, col_q, col_k, col_v):
    b, s, _ = proj3.shape
    nh = N_HEADS_A
    blk = lambda col: pl.BlockSpec((1, s, HEAD_DIM), lambda i, h, col=col: (i, 0, col + h))
    tab = pl.BlockSpec((1, s, HEAD_DIM), lambda i, h: (i, 0, 0))
    gain = pl.BlockSpec((1, HEAD_DIM), lambda i, h: (0, 0))
    return pl.pallas_call(
        _attn_a_kernel,
        grid=(b, nh),
        in_specs=[blk(col_q), blk(col_k), blk(col_v), tab, tab, gain, gain],
        out_specs=pl.BlockSpec((1, s, HEAD_DIM), lambda i, h: (i, 0, h)),
        out_shape=jax.ShapeDtypeStruct((b, s, nh * HEAD_DIM), F32),
        scratch_shapes=[pltpu.VMEM((s, HEAD_DIM), F32), pltpu.VMEM((s, HEAD_DIM), F32),
                        pltpu.VMEM((s, HEAD_DIM), F32), pltpu.VMEM((s, 1), F32),
                        pltpu.VMEM((s, 1), F32)],
        compiler_params=_params(("arbitrary", "arbitrary")),
        name="attn_dilated",
    )(proj3, proj3, proj3, cos2, sin2, g_q.reshape(1, HEAD_DIM), g_k.reshape(1, HEAD_DIM))


QBLK_B = 256
KWIN_B = QBLK_B + 2 * HALF_WINDOW_B
BLOCKS_PER_ITER_B = 2


def _attn_b_kernel(sink_ref, q_ref, k_ref, v_ref, cos_ref, sin_ref, gq_ref, gk_ref, o_ref,
                   qn_ref, kn_ref):
    s = qn_ref.shape[0]
    sink = sink_ref[pl.program_id(1)]
    _norm_rope_into(q_ref, gq_ref, cos_ref, sin_ref, qn_ref, 1.0 / (HEAD_DIM ** 0.5))
    _norm_rope_into(k_ref, gk_ref, cos_ref, sin_ref, kn_ref, 1.0)

    def body(i, _):
        for j in range(BLOCKS_PER_ITER_B):
            a = (i * BLOCKS_PER_ITER_B + j) * QBLK_B
            k0 = jnp.clip(a - HALF_WINDOW_B, 0, s - KWIN_B)
            q = qn_ref[pl.ds(a, QBLK_B), :].astype(BF16)
            k = kn_ref[pl.ds(k0, KWIN_B), :].astype(BF16)
            v = v_ref[0, pl.ds(k0, KWIN_B), :].astype(BF16)
            sc = _dot_nt(q, k)
            sc = jnp.where(_band_mask(a, k0, QBLK_B, KWIN_B, HALF_WINDOW_B), sc, NEG_INF)
            m = jnp.maximum(jnp.max(sc, axis=-1, keepdims=True), sink)
            p = jnp.exp(sc - m)
            l = jnp.sum(p, axis=-1, keepdims=True) + jnp.exp(sink - m)
            o_ref[0, pl.ds(a, QBLK_B), :] = _dot(p.astype(BF16), v) / l
        return 0
    lax.fori_loop(0, s // (QBLK_B * BLOCKS_PER_ITER_B), body, 0)


def _attn_b(proj3, cos2, sin2, g_q, g_k, sink, col_q, col_k, col_v):
    b, s, _ = proj3.shape
    nh = N_HEADS_B
    qblk = pl.BlockSpec((1, s, HEAD_DIM), lambda i, h: (i, 0, col_q + h))
    kblk = pl.BlockSpec((1, s, HEAD_DIM), lambda i, h: (i, 0, col_k + h // GQA_REP))
    vblk = pl.BlockSpec((1, s, HEAD_DIM), lambda i, h: (i, 0, col_v + h // GQA_REP))
    tab = pl.BlockSpec((1, s, HEAD_DIM), lambda i, h: (i, 0, 0))
    gain = pl.BlockSpec((1, HEAD_DIM), lambda i, h: (0, 0))
    return pl.pallas_call(
        _attn_b_kernel,
        grid=(b, nh),
        in_specs=[pl.BlockSpec(memory_space=pltpu.SMEM), qblk, kblk, vblk, tab, tab, gain, gain],
        out_specs=pl.BlockSpec((1, s, HEAD_DIM), lambda i, h: (i, 0, h)),
        out_shape=jax.ShapeDtypeStruct((b, s, nh * HEAD_DIM), F32),
        scratch_shapes=[pltpu.VMEM((s, HEAD_DIM), F32), pltpu.VMEM((s, HEAD_DIM), F32)],
        compiler_params=_params(("arbitrary", "arbitrary")),
        name="attn_window_gqa",
    )(sink, proj3, proj3, proj3, cos2, sin2, g_q.reshape(1, HEAD_DIM), g_k.reshape(1, HEAD_DIM))


def _out_proj_kernel(oa_ref, ob_ref, ga_ref, gb_ref, w_ref, x_ref, o_ref, mix_ref):
    wa = oa_ref.shape[1]

    @pl.when(pl.program_id(1) == 0)
    def _():
        def body(c, _):
            rows = pl.ds(c * ROW_CHUNK, ROW_CHUNK)
            mix_ref[rows, :wa] = _rms(oa_ref[rows, :], ga_ref[...]).astype(BF16)
            mix_ref[rows, wa:] = _rms(ob_ref[rows, :], gb_ref[...]).astype(BF16)
            return 0
        lax.fori_loop(0, oa_ref.shape[0] // ROW_CHUNK, body, 0)

    o_ref[...] = x_ref[...] + _dot(mix_ref[...], w_ref[...].astype(BF16))


def _out_proj(oa, ob, g_oa, g_ob, w_out, x, tm, tn):
    t, wa = oa.shape
    wb = ob.shape[1]
    k, n = w_out.shape
    return pl.pallas_call(
        _out_proj_kernel,
        grid=(t // tm, n // tn),
        in_specs=[pl.BlockSpec((tm, wa), lambda i, j: (i, 0)),
                  pl.BlockSpec((tm, wb), lambda i, j: (i, 0)),
                  pl.BlockSpec((1, wa), lambda i, j: (0, 0)),
                  pl.BlockSpec((1, wb), lambda i, j: (0, 0)),
                  pl.BlockSpec((k, tn), lambda i, j: (0, j)),
                  pl.BlockSpec((tm, tn), lambda i, j: (i, j))],
        out_specs=pl.BlockSpec((tm, tn), lambda i, j: (i, j)),
        out_shape=jax.ShapeDtypeStruct((t, n), F32),
        scratch_shapes=[pltpu.VMEM((tm, k), BF16)],
        compiler_params=_params(("arbitrary", "arbitrary")),
        name="mix_out_proj",
    )(oa, ob, g_oa.reshape(1, wa), g_ob.reshape(1, wb), w_out, x)


def _mem_kv_kernel(m_ref, g_ref, w_ref, gk_ref, k_ref, v_ref):
    wm = k_ref.shape[2]
    mn = _rms(m_ref[0], g_ref[...]).astype(BF16)
    kv = _dot(mn, w_ref[...].astype(BF16))
    for h in range(wm // HEAD_DIM):
        cols = slice(h * HEAD_DIM, (h + 1) * HEAD_DIM)
        k_ref[0, :, cols] = _rms(kv[:, cols], gk_ref[...]).astype(BF16)
    v_ref[0] = kv[:, wm:].astype(BF16)


def _mem_kv(mem, g_mem, w_kv, g_km):
    b, nm, d = mem.shape
    wm = w_kv.shape[1] // 2
    return pl.pallas_call(
        _mem_kv_kernel,
        grid=(b,),
        in_specs=[pl.BlockSpec((1, nm, d), lambda i: (i, 0, 0)),
                  pl.BlockSpec((1, d), lambda i: (0, 0)),
                  pl.BlockSpec((d, 2 * wm), lambda i: (0, 0)),
                  pl.BlockSpec((1, HEAD_DIM), lambda i: (0, 0))],
        out_specs=[pl.BlockSpec((1, nm, wm), lambda i: (i, 0, 0))] * 2,
        out_shape=[jax.ShapeDtypeStruct((b, nm, wm), BF16)] * 2,
        compiler_params=_params(("arbitrary",)),
        name="mem_kv_proj",
    )(mem, g_mem.reshape(1, d), w_kv, g_km.reshape(1, HEAD_DIM))


def _cross_kernel(x_ref, g_ref, wq_ref, gq_ref, k_ref, v_ref, wo_ref, o_ref, wq16_ref, wo16_ref):
    @pl.when(pl.program_id(0) == 0)
    def _():
        wq16_ref[...] = wq_ref[...].astype(BF16)
        wo16_ref[...] = wo_ref[...].astype(BF16)

    x = x_ref[...]
    hn = _rms(x, g_ref[...]).astype(BF16)
    q = _dot(hn, wq16_ref[...])
    outs = []
    for h in range(q.shape[1] // HEAD_DIM):
        cols = slice(h * HEAD_DIM, (h + 1) * HEAD_DIM)
        qh = (_rms(q[:, cols], gq_ref[...]) * (1.0 / (HEAD_DIM ** 0.5))).astype(BF16)
        sc = _dot_nt(qh, k_ref[0, :, cols])
        m = jnp.max(sc, axis=-1, keepdims=True)
        p = jnp.exp(sc - m)
        l = jnp.sum(p, axis=-1, keepdims=True)
        outs.append((_dot(p.astype(BF16), v_ref[0, :, cols]) / l).astype(BF16))
    o = jnp.concatenate(outs, axis=-1)
    o_ref[...] = x + _dot(o, wo16_ref[...])


def _cross_attn(x, g_cross, w_q, g_qm, kmem, vmem, w_o, s, tm):
    t, d = x.shape
    wm = w_q.shape[1]
    nm = kmem.shape[1]
    per_b = s // tm
    return pl.pallas_call(
        _cross_kernel,
        grid=(t // tm,),
        in_specs=[pl.BlockSpec((tm, d), lambda i: (i, 0)),
                  pl.BlockSpec((1, d), lambda i: (0, 0)),
                  pl.BlockSpec((d, wm), lambda i: (0, 0)),
                  pl.BlockSpec((1, HEAD_DIM), lambda i: (0, 0)),
                  pl.BlockSpec((1, nm, wm), lambda i: (i // per_b, 0, 0)),
                  pl.BlockSpec((1, nm, wm), lambda i: (i // per_b, 0, 0)),
                  pl.BlockSpec((wm, d), lambda i: (0, 0))],
        out_specs=pl.BlockSpec((tm, d), lambda i: (i, 0)),
        out_shape=jax.ShapeDtypeStruct((t, d), F32),
        scratch_shapes=[pltpu.VMEM((d, wm), BF16), pltpu.VMEM((wm, d), BF16)],
        compiler_params=_params(("arbitrary",)),
        name="mem_cross_attn",
    )(x, g_cross.reshape(1, d), w_q, g_qm.reshape(1, HEAD_DIM), kmem, vmem, w_o)


def _router_kernel(x_ref, g_ref, wr_ref, h_ref, aff_ref, *, n_experts):
    h = _rms(x_ref[...], g_ref[...])
    h_ref[...] = h
    logits = jnp.dot(h, wr_ref[...], preferred_element_type=F32, precision=lax.Precision.HIGHEST)
    lane = lax.broadcasted_iota(jnp.int32, logits.shape, 1)
    logits = jnp.where(lane < n_experts, logits, NEG_INF)
    m = jnp.max(logits, axis=-1, keepdims=True)
    e = jnp.exp(logits - m)
    aff_ref[...] = e / jnp.sum(e, axis=-1, keepdims=True)


def _router(x, g_moe, w_router, tm):
    t, d = x.shape
    ne = w_router.shape[1]
    wr = jnp.pad(w_router, ((0, 0), (0, LANES - ne)))
    return pl.pallas_call(
        functools.partial(_router_kernel, n_experts=ne),
        grid=(t // tm,),
        in_specs=[pl.BlockSpec((tm, d), lambda i: (i, 0)),
                  pl.BlockSpec((1, d), lambda i: (0, 0)),
                  pl.BlockSpec((d, LANES), lambda i: (0, 0))],
        out_specs=[pl.BlockSpec((tm, d), lambda i: (i, 0)),
                   pl.BlockSpec((tm, LANES), lambda i: (i, 0))],
        out_shape=[jax.ShapeDtypeStruct((t, d), F32), jax.ShapeDtypeStruct((t, LANES), F32)],
        compiler_params=_params(("arbitrary",)),
        name="moe_router",
    )(x, g_moe.reshape(1, d), wr)


PREFIX_BLK = 256
SEL_CHUNK = 256


def _prefix_rows_into(mask_ref, out_ref, ltri):
    carry = jnp.zeros((1, mask_ref.shape[1]), F32)
    for blk in range(mask_ref.shape[0] // PREFIX_BLK):
        rows = pl.ds(blk * PREFIX_BLK, PREFIX_BLK)
        mb = mask_ref[rows, :]
        out_ref[rows, :] = _dot(ltri, mb.astype(BF16)) + carry
        carry = carry + jnp.sum(mb, axis=0, keepdims=True)


def _select_kernel(aff_ref, idx_ref, gate_ref, bits_ref, mask_ref, pre_ref, key_ref,
                   *, n_experts, cap):
    s = aff_ref.shape[1]
    bits_ref[...] = pltpu.bitcast(aff_ref[0], jnp.int32)

    def search(i, tau):
        cand = tau | jnp.left_shift(jnp.int32(1), 30 - i)
        cnt = jnp.sum((bits_ref[...] >= cand).astype(F32), axis=0, keepdims=True)
        return jnp.where(cnt >= cap, cand, tau)
    tau = lax.fori_loop(0, 31, search, jnp.zeros((1, LANES), jnp.int32))

    r = lax.broadcasted_iota(jnp.int32, (PREFIX_BLK, PREFIX_BLK), 0)
    c = lax.broadcasted_iota(jnp.int32, (PREFIX_BLK, PREFIX_BLK), 1)
    ltri = (c < r).astype(BF16)

    gt = bits_ref[...] > tau
    n_gt = jnp.sum(gt.astype(F32), axis=0, keepdims=True)
    mask_ref[...] = (bits_ref[...] == tau).astype(F32)
    _prefix_rows_into(mask_ref, pre_ref, ltri)
    sel = gt | ((mask_ref[...] > 0.5) & (pre_ref[...] < cap - n_gt))
    mask_ref[...] = sel.astype(F32)
    _prefix_rows_into(mask_ref, pre_ref, ltri)
    key_ref[...] = jnp.where(mask_ref[...] > 0.5, pre_ref[...], -1.0)

    slot_iota = lax.broadcasted_iota(jnp.int32, (SEL_CHUNK, cap), 1).astype(F32)
    row_iota = lax.broadcasted_iota(jnp.int32, (SEL_CHUNK, cap), 0).astype(F32)
    base = pl.program_id(0) * s
    for e in range(n_experts):
        def body(ch, carry, e=e):
            idx_acc, gate_acc = carry
            rows = pl.ds(ch * SEL_CHUNK, SEL_CHUNK)
            hit = key_ref[rows, e:e + 1] == slot_iota
            tok = row_iota + jnp.asarray(base + ch * SEL_CHUNK, F32)
            idx_c = jnp.where(hit, tok, 0.0)
            gate_c = jnp.where(hit, aff_ref[0, rows, e:e + 1], 0.0)
            idx_acc = idx_acc + idx_c.reshape(SEL_CHUNK // 8, 8, cap).sum(axis=0)
            gate_acc = gate_acc + gate_c.reshape(SEL_CHUNK // 8, 8, cap).sum(axis=0)
            return idx_acc, gate_acc
        zero = jnp.zeros((8, cap), F32)
        idx_acc, gate_acc = lax.fori_loop(0, s // SEL_CHUNK, body, (zero, zero))
        idx_ref[0, e:e + 1, :] = jnp.sum(idx_acc, axis=0, keepdims=True).astype(jnp.int32)
        gate_ref[0, e:e + 1, :] = jnp.sum(gate_acc, axis=0, keepdims=True)


def _select(aff3, n_experts, cap):
    b, s, _ = aff3.shape
    return pl.pallas_call(
        functools.partial(_select_kernel, n_experts=n_experts, cap=cap),
        grid=(b,),
        in_specs=[pl.BlockSpec((1, s, LANES), lambda i: (i, 0, 0))],
        out_specs=[pl.BlockSpec((1, n_experts, cap), lambda i: (i, 0, 0))] * 2,
        out_shape=[jax.ShapeDtypeStruct((b, n_experts, cap), jnp.int32),
                   jax.ShapeDtypeStruct((b, n_experts, cap), F32)],
        scratch_shapes=[pltpu.VMEM((s, LANES), jnp.int32), pltpu.VMEM((s, LANES), F32),
                        pltpu.VMEM((s, LANES), F32), pltpu.VMEM((s, LANES), F32)],
        compiler_params=_params(("arbitrary",)),
        name="moe_select",
    )(aff3)


DOWN_COLS = 512


def _moe_kernel(idx_ref, h_hbm, x_hbm, gate_ref, gprev_ref, wg_ref, wu_ref, wd_ref, o_hbm,
                xe32_ref, xe16_ref, y_ref, row_ref, sem_ref, bar_ref, *, rows, n_experts, n_f):
    del x_hbm
    e = pl.program_id(0)
    f = pl.program_id(1)
    ch = rows // n_f
    gather_sem, scatter_sem = sem_ref.at[0], sem_ref.at[1]
    acc_sem = lambda c: sem_ref.at[2 + c % 2]
    slot = e % 2
    pe = jnp.maximum(e - 1, 0)
    valid = jnp.where(e > 0, 1.0, 0.0).astype(F32)

    def h_row_in(t, i):
        return pltpu.make_async_copy(h_hbm.at[pl.ds(t, 1)], xe32_ref.at[pl.ds(i, 1)], gather_sem)

    def acc_row_in(c):
        return lambda t, i: pltpu.make_async_copy(o_hbm.at[pl.ds(t, 1)], row_ref.at[c, pl.ds(i, 1)],
                                                  acc_sem(c))

    def acc_row_out(c):
        return lambda t, i: pltpu.make_async_copy(row_ref.at[c, pl.ds(i, 1)], o_hbm.at[pl.ds(t, 1)],
                                                  scatter_sem)

    def start_unrolled(list_base, n, make, dst_base=0):
        for i in range(n):
            make(idx_ref[list_base + i], dst_base + i).start()

    def start_looped(list_base, n, make):
        def body(i, _):
            make(idx_ref[list_base + i], i).start()
            return 0
        lax.fori_loop(0, n, body, 0)

    def wait_rows(n_rows, sem):
        pltpu.make_async_copy(h_hbm.at[pl.ds(0, n_rows)], xe32_ref.at[pl.ds(0, n_rows)], sem).wait()

    def gated_rows(c, y_slot, g_ref, scale):
        crows = pl.ds(c * ch, ch)
        return row_ref[c] + y_ref[y_slot, crows, :] * (g_ref[0, crows, 0:1] * scale)

    @pl.when((e == 0) & (f == 0))
    def _():
        start_looped(0, rows, h_row_in)
        y_ref[1] = jnp.zeros(y_ref.shape[1:], F32)
        start_looped(0, ch, acc_row_in(0))

    def step(f):
        first, last = f == 0, f == n_f - 1
        wait_rows(ch, acc_sem(f))
        row_ref[f] = gated_rows(f, 1 - slot, gprev_ref, valid)
        pl.semaphore_signal(bar_ref, 1)
        pl.semaphore_wait(bar_ref, 1)
        if first:
            wait_rows(rows, gather_sem)
            xe16_ref[...] = xe32_ref[...].astype(BF16)
        start_unrolled(pe * rows + f * ch, ch, acc_row_out(f))

        xe = xe16_ref[...]
        g = _dot(xe, wg_ref[0].astype(BF16))
        start_unrolled(jnp.minimum(e + 1, n_experts - 1) * rows + f * ch, ch, h_row_in, f * ch)
        u = _dot(xe, wu_ref[0].astype(BF16))
        act = (g * (1.0 / (1.0 + jnp.exp(-g))) * u).astype(BF16)
        for n0 in range(0, y_ref.shape[2], DOWN_COLS):
            cols = slice(n0, n0 + DOWN_COLS)
            part = _dot(act, wd_ref[0, :, cols].astype(BF16))
            if first:
                y_ref[slot, :, cols] = part
            else:
                y_ref[slot, :, cols] += part

        if last:
            wait_rows(rows, scatter_sem)
            start_unrolled(e * rows, ch, acc_row_in(0))
        else:
            start_unrolled(pe * rows + (f + 1) * ch, ch, acc_row_in(f + 1))

    for fs in range(n_f):
        pl.when(f == fs)(functools.partial(step, fs))

    @pl.when((e == n_experts - 1) & (f == n_f - 1))
    def _():
        wait_rows(rows, gather_sem)
        for c in range(n_f):
            wait_rows(ch, acc_sem(c))
            if c + 1 < n_f:
                start_looped(e * rows + (c + 1) * ch, ch, acc_row_in(c + 1))
            row_ref[c] = gated_rows(c, slot, gate_ref, 1.0)
            start_looped(e * rows + c * ch, ch, acc_row_out(c))
        wait_rows(rows, scatter_sem)


def _moe_experts(idx_flat, gate_rows, h, x, w_gate, w_up, w_down, tf):
    t, d = x.shape
    ne, _, dff = w_gate.shape
    rows = idx_flat.shape[0] // ne
    n_f = dff // tf
    assert n_f >= 2
    grid_spec = pltpu.PrefetchScalarGridSpec(
        num_scalar_prefetch=1,
        grid=(ne, n_f),
        in_specs=[pl.BlockSpec(memory_space=pl.ANY),
                  pl.BlockSpec(memory_space=pl.ANY),
                  pl.BlockSpec((1, rows, LANES), lambda e, f, idx: (e, 0, 0)),
                  pl.BlockSpec((1, rows, LANES), lambda e, f, idx: (jnp.maximum(e - 1, 0), 0, 0)),
                  pl.BlockSpec((1, d, tf), lambda e, f, idx: (e, 0, f)),
                  pl.BlockSpec((1, d, tf), lambda e, f, idx: (e, 0, f)),
                  pl.BlockSpec((1, tf, d), lambda e, f, idx: (e, f, 0))],
        out_specs=pl.BlockSpec(memory_space=pl.ANY),
        scratch_shapes=[pltpu.VMEM((rows, d), F32), pltpu.VMEM((rows, d), BF16),
                        pltpu.VMEM((2, rows, d), F32), pltpu.VMEM((n_f, rows // n_f, d), F32),
                        pltpu.SemaphoreType.DMA((4,)), pltpu.SemaphoreType.REGULAR],
    )
    return pl.pallas_call(
        functools.partial(_moe_kernel, rows=rows, n_experts=ne, n_f=n_f),
        grid_spec=grid_spec,
        out_shape=jax.ShapeDtypeStruct((t, d), F32),
        input_output_aliases={2: 0},
        compiler_params=pltpu.CompilerParams(dimension_semantics=("arbitrary", "arbitrary"),
                                             vmem_limit_bytes=MOE_VMEM_LIMIT),
        name="moe_experts",
    )(idx_flat, h, x, gate_rows, gate_rows, w_gate, w_up, w_down)


def _moe(x, s, g_moe, w_router, w_gate, w_up, w_down):
    t, d = x.shape
    b = t // s
    ne = w_router.shape[1]
    cap = EC_CAPACITY * s // ne
    h, aff = _router(x, g_moe, w_router, tm=512)
    idx, gate = _select(aff.reshape(b, s, LANES), ne, cap)
    idx_flat = idx.transpose(1, 0, 2).reshape(ne * b * cap)
    gate_rows = jnp.broadcast_to(gate.transpose(1, 0, 2).reshape(ne, b * cap, 1), (ne, b * cap, LANES))
    return _moe_experts(idx_flat, gate_rows, h, x, w_gate, w_up, w_down, tf=256)


def kernel(x, mem, positions, g_mix, w_in, g_qa, g_ka, g_qb, g_kb, sink_b, g_oa, g_ob, w_out,
           g_cross, g_mem, w_q_mem, w_kv_mem, g_qm, g_km, w_o_mem, g_moe, w_router, w_gate,
           w_up, w_down):
    b, s, d = x.shape
    t = b * s
    depth = g_mix.shape[0]
    width_a = N_HEADS_A * HEAD_DIM
    width_b = N_HEADS_B * HEAD_DIM
    kv_b = width_b // GQA_REP
    col = [0, width_a, 2 * width_a, 3 * width_a, 3 * width_a + width_b, 3 * width_a + width_b + kv_b]
    col = [c // HEAD_DIM for c in col]

    cos2, sin2 = _rope_tables(positions)
    xf = x.reshape(t, d)
    for l in range(depth):
        proj = _norm_matmul(xf, g_mix[l], w_in[l], tm=1024, tn=512)
        proj3 = proj.reshape(b, s, proj.shape[1])
        oa = _attn_a(proj3, cos2, sin2, g_qa[l], g_ka[l], col[0], col[1], col[2])
        ob = _attn_b(proj3, cos2, sin2, g_qb[l], g_kb[l], sink_b[l], col[3], col[4], col[5])
        xf = _out_proj(oa.reshape(t, width_a), ob.reshape(t, width_b), g_oa[l], g_ob[l], w_out[l],
                       xf, tm=1024, tn=512)
        kmem, vmem = _mem_kv(mem, g_mem[l], w_kv_mem[l], g_km[l])
        xf = _cross_attn(xf, g_cross[l], w_q_mem[l], g_qm[l], kmem, vmem, w_o_mem[l], s, tm=512)
        xf = _moe(xf, s, g_moe[l], w_router[l], w_gate[l], w_up[l], w_down[l])
    return xf.reshape(b, s, d)
```

```python
import functools

import jax
import jax.numpy as jnp
from jax import lax
from jax.experimental import pallas as pl
from jax.experimental.pallas import tpu as pltpu

F32 = jnp.float32
BF16 = jnp.bfloat16

HEAD_DIM = 128
LANES = 128
EPS = 1e-6
NEG_INF = -1e30
ROPE_THETA = 10000.0
DILATED_GROUPS = ((128, 1), (512, 4), (2048, 16))
HALF_WINDOW_B = 128
GQA_REP = 4
N_HEADS_A = 8
N_HEADS_B = 8
N_HEADS_MEM = 4
N_EXPERTS = 16
EC_CAPACITY = 2
VMEM_LIMIT = 56 * 1024 * 1024
MOE_VMEM_LIMIT = 60 * 1024 * 1024


def _params(sem):
    return pltpu.CompilerParams(dimension_semantics=sem, vmem_limit_bytes=VMEM_LIMIT)


def _rms(x, g):
    return x * lax.rsqrt(jnp.mean(x * x, axis=-1, keepdims=True) + EPS) * g


def _dot(a, b):
    return jnp.dot(a, b, preferred_element_type=F32)


def _dot_nt(a, b):
    return lax.dot_general(a, b, (((1,), (1,)), ((), ())), preferred_element_type=F32)


def _rope_kernel(pos_ref, invf_ref, cos_ref, sin_ref):
    ang = pos_ref[...] * invf_ref[...]
    lane = lax.broadcasted_iota(jnp.int32, ang.shape, 1)
    cos_ref[...] = jnp.cos(ang)
    sin_ref[...] = jnp.where(lane < HEAD_DIM // 2, -jnp.sin(ang), jnp.sin(ang))


def _rope_tables(positions):
    b, s = positions.shape
    t = b * s
    inv_freq = ROPE_THETA ** (-jnp.arange(0, HEAD_DIM, 2, dtype=F32) / HEAD_DIM)
    invf2 = jnp.concatenate([inv_freq, inv_freq]).reshape(1, HEAD_DIM)
    pos = jnp.broadcast_to(positions.reshape(t, 1).astype(F32), (t, HEAD_DIM))
    ts = 1024
    cos2, sin2 = pl.pallas_call(
        _rope_kernel,
        grid=(t // ts,),
        in_specs=[pl.BlockSpec((ts, HEAD_DIM), lambda i: (i, 0)),
                  pl.BlockSpec((1, HEAD_DIM), lambda i: (0, 0))],
        out_specs=[pl.BlockSpec((ts, HEAD_DIM), lambda i: (i, 0))] * 2,
        out_shape=[jax.ShapeDtypeStruct((t, HEAD_DIM), F32)] * 2,
        compiler_params=_params(("arbitrary",)),
        name="rope_tables",
    )(pos, invf2)
    return cos2.reshape(b, s, HEAD_DIM), sin2.reshape(b, s, HEAD_DIM)


ROW_CHUNK = 256
ROPE_CHUNK = 1024


def _norm_matmul_kernel(x_ref, g_ref, w_ref, o_ref, xn_ref):
    @pl.when(pl.program_id(1) == 0)
    def _():
        def body(c, _):
            rows = pl.ds(c * ROW_CHUNK, ROW_CHUNK)
            xn_ref[rows, :] = _rms(x_ref[rows, :], g_ref[...]).astype(BF16)
            return 0
        lax.fori_loop(0, x_ref.shape[0] // ROW_CHUNK, body, 0)

    o_ref[...] = _dot(xn_ref[...], w_ref[...].astype(BF16))


def _norm_matmul(x, g, w, tm, tn):
    t, k = x.shape
    n = w.shape[1]
    return pl.pallas_call(
        _norm_matmul_kernel,
        grid=(t // tm, n // tn),
        in_specs=[pl.BlockSpec((tm, k), lambda i, j: (i, 0)),
                  pl.BlockSpec((1, k), lambda i, j: (0, 0)),
                  pl.BlockSpec((k, tn), lambda i, j: (0, j))],
        out_specs=pl.BlockSpec((tm, tn), lambda i, j: (i, j)),
        out_shape=jax.ShapeDtypeStruct((t, n), F32),
        scratch_shapes=[pltpu.VMEM((tm, k), BF16)],
        compiler_params=_params(("arbitrary", "arbitrary")),
        name="norm_in_proj",
    )(x, g.reshape(1, k), w)


def _norm_rope_into(src_ref, g_ref, cos_ref, sin_ref, dst_ref, scale):
    def body(c, _):
        rows = pl.ds(c * ROPE_CHUNK, ROPE_CHUNK)
        t = _rms(src_ref[0, rows, :], g_ref[...])
        t = t * cos_ref[0, rows, :] + pltpu.roll(t, HEAD_DIM // 2, axis=1) * sin_ref[0, rows, :]
        dst_ref[rows, :] = t * scale
        return 0
    lax.fori_loop(0, dst_ref.shape[0] // ROPE_CHUNK, body, 0)


def _band_mask(q0, k0, nq, nk, half):
    qpos = q0 + lax.broadcasted_iota(jnp.int32, (nq, nk), 0)
    kpos = k0 + lax.broadcasted_iota(jnp.int32, (nq, nk), 1)
    return jnp.abs(qpos - kpos) <= half


QBLK_A = 128
KWIN_A = 256
QROWS_PER_ITER_A = 512


def _rows(start, size, stride):
    return pl.ds(start, size) if stride == 1 else pl.ds(start, size, stride=stride)


def _attn_a_kernel(q_ref, k_ref, v_ref, cos_ref, sin_ref, gq_ref, gk_ref, o_ref,
                   qn_ref, kn_ref, acc_ref, ml_ref):
    s = qn_ref.shape[0]
    _norm_rope_into(q_ref, gq_ref, cos_ref, sin_ref, qn_ref, 1.0 / (HEAD_DIM ** 0.5))
    _norm_rope_into(k_ref, gk_ref, cos_ref, sin_ref, kn_ref, 1.0)

    groups = sorted(DILATED_GROUPS, key=lambda wd: -wd[1])
    for gi, (w, d) in enumerate(groups):
        half = w // (2 * d)
        ln = s // d
        whole = ln <= KWIN_A
        qblk = ln if whole else QBLK_A
        kwin = ln if whole else KWIN_A
        nblk = ln // qblk
        per_iter = QROWS_PER_ITER_A // qblk

        def body(it, _, d=d, half=half, ln=ln, nblk=nblk, qblk=qblk, kwin=kwin, per_iter=per_iter,
                 first=(gi == 0)):
            blocks = []
            for j in range(per_iter):
                n = it * per_iter + j
                r = n // nblk
                a = (n % nblk) * qblk
                k0 = jnp.clip(a - (kwin - qblk) // 2, 0, ln - kwin)
                qrows = _rows(r + d * a, qblk, d)
                krows = _rows(r + d * k0, kwin, d)
                q = qn_ref[qrows, :].astype(BF16)
                k = kn_ref[krows, :].astype(BF16)
                v = v_ref[0, krows, :].astype(BF16)
                old = None if first else (ml_ref[qrows, :], acc_ref[qrows, :])
                blocks.append((a, k0, qrows, q, k, v, old))
            scores = [_dot_nt(q, k) for _, _, _, q, k, _, _ in blocks]
            probs = []
            for (a, k0, _, _, _, _, old), sc in zip(blocks, scores):
                sc = jnp.where(_band_mask(a, k0, qblk, kwin, half), sc, NEG_INF)
                m_new = jnp.max(sc, axis=-1, keepdims=True)
                if old is not None:
                    m_new = jnp.maximum(old[0][:, 0:1], m_new)
                p = jnp.exp(sc - m_new)
                probs.append((m_new, jnp.sum(p, axis=-1, keepdims=True), p.astype(BF16)))
            results = []
            for (_, _, qrows, _, _, v, old), (m_new, l_new, p) in zip(blocks, probs):
                acc_new = _dot(p, v)
                if old is not None:
                    alpha = jnp.exp(old[0][:, 0:1] - m_new)
                    l_new = alpha * old[0][:, 1:2] + l_new
                    acc_new = alpha * old[1] + acc_new
                results.append((qrows, m_new, l_new, acc_new))
            lane = lax.broadcasted_iota(jnp.int32, (qblk, ml_ref.shape[1]), 1)
            for qrows, m_new, l_new, acc_new in results:
                ml_ref[qrows, :] = jnp.where(lane == 0, m_new, l_new)
                acc_ref[qrows, :] = acc_new
            return 0

        lax.fori_loop(0, d * nblk // per_iter, body, 0)

    def fin(c, _):
        rows = pl.ds(c * ROW_CHUNK, ROW_CHUNK)
        o_ref[0, rows, :] = acc_ref[rows, :] / ml_ref[rows, 1:2]
        return 0
    lax.fori_loop(0, s // ROW_CHUNK, fin, 0)


def _attn_a(proj3, cos2, sin2, g_q, g_k, col_q, col_k, col_v):
    b, s, _ = proj3.shape
    nh = N_HEADS_A
    blk = lambda col: pl.BlockSpec((1, s, HEAD_DIM), lambda i, h, col=col: (i, 0, col + h))
    tab = pl.BlockSpec((1, s, HEAD_DIM), lambda i, h: (i, 0, 0))
    gain = pl.BlockSpec((1, HEAD_DIM), lambda i, h: (0, 0))
    return pl.pallas_call(
        _attn_a_kernel,
        grid=(b, nh),
        in_specs=[blk(col_q), blk(col_k), blk(col_v), tab, tab, gain, gain],
        out_specs=pl.BlockSpec((1, s, HEAD_DIM), lambda i, h: (i, 0, h)),
        out_shape=jax.ShapeDtypeStruct((b, s, nh * HEAD_DIM), F32),
        scratch_shapes=[pltpu.VMEM((s, HEAD_DIM), F32), pltpu.VMEM((s, HEAD_DIM), F32),
                        pltpu.VMEM((s, HEAD_DIM), F32), pltpu.VMEM((s, LANES), F32)],
        compiler_params=_params(("arbitrary", "arbitrary")),
        name="attn_dilated",
    )(proj3, proj3, proj3, cos2, sin2, g_q.reshape(1, HEAD_DIM), g_k.reshape(1, HEAD_DIM))


QBLK_B = 256
KWIN_B = QBLK_B + 2 * HALF_WINDOW_B
BLOCKS_PER_ITER_B = 2


def _attn_b_kernel(sink_ref, q_ref, k_ref, v_ref, cos_ref, sin_ref, gq_ref, gk_ref, o_ref,
                   qn_ref, kn_ref):
    s = qn_ref.shape[0]
    sink = sink_ref[pl.program_id(1)]
    _norm_rope_into(q_ref, gq_ref, cos_ref, sin_ref, qn_ref, 1.0 / (HEAD_DIM ** 0.5))
    _norm_rope_into(k_ref, gk_ref, cos_ref, sin_ref, kn_ref, 1.0)

    def body(i, _):
        blocks = []
        for j in range(BLOCKS_PER_ITER_B):
            a = (i * BLOCKS_PER_ITER_B + j) * QBLK_B
            k0 = jnp.clip(a - HALF_WINDOW_B, 0, s - KWIN_B)
            q = qn_ref[pl.ds(a, QBLK_B), :].astype(BF16)
            k = kn_ref[pl.ds(k0, KWIN_B), :].astype(BF16)
            v = v_ref[0, pl.ds(k0, KWIN_B), :].astype(BF16)
            blocks.append((a, k0, v, _dot_nt(q, k)))
        probs = []
        for a, k0, v, sc in blocks:
            sc = jnp.where(_band_mask(a, k0, QBLK_B, KWIN_B, HALF_WINDOW_B), sc, NEG_INF)
            m = jnp.maximum(jnp.max(sc, axis=-1, keepdims=True), sink)
            p = jnp.exp(sc - m)
            l = jnp.sum(p, axis=-1, keepdims=True) + jnp.exp(sink - m)
            probs.append((a, v, p.astype(BF16), l))
        for a, v, p, l in probs:
            o_ref[0, pl.ds(a, QBLK_B), :] = _dot(p, v) / l
        return 0
    lax.fori_loop(0, s // (QBLK_B * BLOCKS_PER_ITER_B), body, 0)


def _attn_b(proj3, cos2, sin2, g_q, g_k, sink, col_q, col_k, col_v):
    b, s, _ = proj3.shape
    nh = N_HEADS_B
    qblk = pl.BlockSpec((1, s, HEAD_DIM), lambda i, h: (i, 0, col_q + h))
    kblk = pl.BlockSpec((1, s, HEAD_DIM), lambda i, h: (i, 0, col_k + h // GQA_REP))
    vblk = pl.BlockSpec((1, s, HEAD_DIM), lambda i, h: (i, 0, col_v + h // GQA_REP))
    tab = pl.BlockSpec((1, s, HEAD_DIM), lambda i, h: (i, 0, 0))
    gain = pl.BlockSpec((1, HEAD_DIM), lambda i, h: (0, 0))
    return pl.pallas_call(
        _attn_b_kernel,
        grid=(b, nh),
        in_specs=[pl.BlockSpec(memory_space=pltpu.SMEM), qblk, kblk, vblk, tab, tab, gain, gain],
        out_specs=pl.BlockSpec((1, s, HEAD_DIM), lambda i, h: (i, 0, h)),
        out_shape=jax.ShapeDtypeStruct((b, s, nh * HEAD_DIM), F32),
        scratch_shapes=[pltpu.VMEM((s, HEAD_DIM), F32), pltpu.VMEM((s, HEAD_DIM), F32)],
        compiler_params=_params(("arbitrary", "arbitrary")),
        name="attn_window_gqa",
    )(sink, proj3, proj3, proj3, cos2, sin2, g_q.reshape(1, HEAD_DIM), g_k.reshape(1, HEAD_DIM))


def _out_proj_kernel(oa_ref, ob_ref, ga_ref, gb_ref, w_ref, x_ref, o_ref, mix_ref):
    wa = oa_ref.shape[1]

    @pl.when(pl.program_id(1) == 0)
    def _():
        def body(c, _):
            rows = pl.ds(c * ROW_CHUNK, ROW_CHUNK)
            mix_ref[rows, :wa] = _rms(oa_ref[rows, :], ga_ref[...]).astype(BF16)
            mix_ref[rows, wa:] = _rms(ob_ref[rows, :], gb_ref[...]).astype(BF16)
            return 0
        lax.fori_loop(0, oa_ref.shape[0] // ROW_CHUNK, body, 0)

    o_ref[...] = x_ref[...] + _dot(mix_ref[...], w_ref[...].astype(BF16))


def _out_proj(oa, ob, g_oa, g_ob, w_out, x, tm, tn):
    t, wa = oa.shape
    wb = ob.shape[1]
    k, n = w_out.shape
    return pl.pallas_call(
        _out_proj_kernel,
        grid=(t // tm, n // tn),
        in_specs=[pl.BlockSpec((tm, wa), lambda i, j: (i, 0)),
                  pl.BlockSpec((tm, wb), lambda i, j: (i, 0)),
                  pl.BlockSpec((1, wa), lambda i, j: (0, 0)),
                  pl.BlockSpec((1, wb), lambda i, j: (0, 0)),
                  pl.BlockSpec((k, tn), lambda i, j: (0, j)),
                  pl.BlockSpec((tm, tn), lambda i, j: (i, j))],
        out_specs=pl.BlockSpec((tm, tn), lambda i, j: (i, j)),
        out_shape=jax.ShapeDtypeStruct((t, n), F32),
        scratch_shapes=[pltpu.VMEM((tm, k), BF16)],
        compiler_params=_params(("arbitrary", "arbitrary")),
        name="mix_out_proj",
    )(oa, ob, g_oa.reshape(1, wa), g_ob.reshape(1, wb), w_out, x)


def _mem_kv_kernel(m_ref, g_ref, w_ref, gk_ref, k_ref, v_ref):
    wm = k_ref.shape[2]
    mn = _rms(m_ref[0], g_ref[...]).astype(BF16)
    kv = _dot(mn, w_ref[...].astype(BF16))
    for h in range(wm // HEAD_DIM):
        cols = slice(h * HEAD_DIM, (h + 1) * HEAD_DIM)
        k_ref[0, :, cols] = _rms(kv[:, cols], gk_ref[...]).astype(BF16)
    v_ref[0] = kv[:, wm:].astype(BF16)


def _mem_kv(mem, g_mem, w_kv, g_km):
    b, nm, d = mem.shape
    wm = w_kv.shape[1] // 2
    return pl.pallas_call(
        _mem_kv_kernel,
        grid=(b,),
        in_specs=[pl.BlockSpec((1, nm, d), lambda i: (i, 0, 0)),
                  pl.BlockSpec((1, d), lambda i: (0, 0)),
                  pl.BlockSpec((d, 2 * wm), lambda i: (0, 0)),
                  pl.BlockSpec((1, HEAD_DIM), lambda i: (0, 0))],
        out_specs=[pl.BlockSpec((1, nm, wm), lambda i: (i, 0, 0))] * 2,
        out_shape=[jax.ShapeDtypeStruct((b, nm, wm), BF16)] * 2,
        compiler_params=_params(("arbitrary",)),
        name="mem_kv_proj",
    )(mem, g_mem.reshape(1, d), w_kv, g_km.reshape(1, HEAD_DIM))


def _cross_kernel(x_ref, g_ref, wq_ref, gq_ref, k_ref, v_ref, wo_ref, o_ref, wq16_ref, wo16_ref):
    @pl.when(pl.program_id(0) == 0)
    def _():
        wq16_ref[...] = wq_ref[...].astype(BF16)
        wo16_ref[...] = wo_ref[...].astype(BF16)

    x = x_ref[...]
    hn = _rms(x, g_ref[...]).astype(BF16)
    q = _dot(hn, wq16_ref[...])
    outs = []
    for h in range(q.shape[1] // HEAD_DIM):
        cols = slice(h * HEAD_DIM, (h + 1) * HEAD_DIM)
        qh = (_rms(q[:, cols], gq_ref[...]) * (1.0 / (HEAD_DIM ** 0.5))).astype(BF16)
        sc = _dot_nt(qh, k_ref[0, :, cols])
        m = jnp.max(sc, axis=-1, keepdims=True)
        p = jnp.exp(sc - m)
        l = jnp.sum(p, axis=-1, keepdims=True)
        outs.append((_dot(p.astype(BF16), v_ref[0, :, cols]) / l).astype(BF16))
    o = jnp.concatenate(outs, axis=-1)
    o_ref[...] = x + _dot(o, wo16_ref[...])


def _cross_attn(x, g_cross, w_q, g_qm, kmem, vmem, w_o, s, tm):
    t, d = x.shape
    wm = w_q.shape[1]
    nm = kmem.shape[1]
    per_b = s // tm
    return pl.pallas_call(
        _cross_kernel,
        grid=(t // tm,),
        in_specs=[pl.BlockSpec((tm, d), lambda i: (i, 0)),
                  pl.BlockSpec((1, d), lambda i: (0, 0)),
                  pl.BlockSpec((d, wm), lambda i: (0, 0)),
                  pl.BlockSpec((1, HEAD_DIM), lambda i: (0, 0)),
                  pl.BlockSpec((1, nm, wm), lambda i: (i // per_b, 0, 0)),
                  pl.BlockSpec((1, nm, wm), lambda i: (i // per_b, 0, 0)),
                  pl.BlockSpec((wm, d), lambda i: (0, 0))],
        out_specs=pl.BlockSpec((tm, d), lambda i: (i, 0)),
        out_shape=jax.ShapeDtypeStruct((t, d), F32),
        scratch_shapes=[pltpu.VMEM((d, wm), BF16), pltpu.VMEM((wm, d), BF16)],
        compiler_params=_params(("arbitrary",)),
        name="mem_cross_attn",
    )(x, g_cross.reshape(1, d), w_q, g_qm.reshape(1, HEAD_DIM), kmem, vmem, w_o)


def _router_kernel(x_ref, g_ref, wr_ref, h_ref, aff_ref, *, n_experts):
    h = _rms(x_ref[...], g_ref[...])
    h_ref[...] = h
    logits = jnp.dot(h, wr_ref[...], preferred_element_type=F32, precision=lax.Precision.HIGHEST)
    lane = lax.broadcasted_iota(jnp.int32, logits.shape, 1)
    logits = jnp.where(lane < n_experts, logits, NEG_INF)
    m = jnp.max(logits, axis=-1, keepdims=True)
    e = jnp.exp(logits - m)
    aff_ref[...] = e / jnp.sum(e, axis=-1, keepdims=True)


def _router(x, g_moe, w_router, tm):
    t, d = x.shape
    ne = w_router.shape[1]
    wr = jnp.pad(w_router, ((0, 0), (0, LANES - ne)))
    return pl.pallas_call(
        functools.partial(_router_kernel, n_experts=ne),
        grid=(t // tm,),
        in_specs=[pl.BlockSpec((tm, d), lambda i: (i, 0)),
                  pl.BlockSpec((1, d), lambda i: (0, 0)),
                  pl.BlockSpec((d, LANES), lambda i: (0, 0))],
        out_specs=[pl.BlockSpec((tm, d), lambda i: (i, 0)),
                   pl.BlockSpec((tm, LANES), lambda i: (i, 0))],
        out_shape=[jax.ShapeDtypeStruct((t, d), F32), jax.ShapeDtypeStruct((t, LANES), F32)],
        compiler_params=_params(("arbitrary",)),
        name="moe_router",
    )(x, g_moe.reshape(1, d), wr)


PREFIX_BLK = 256
SEL_CHUNK = 256


def _prefix_rows_into(mask_ref, out_ref, ltri):
    carry = jnp.zeros((1, mask_ref.shape[1]), F32)
    for blk in range(mask_ref.shape[0] // PREFIX_BLK):
        rows = pl.ds(blk * PREFIX_BLK, PREFIX_BLK)
        mb = mask_ref[rows, :]
        out_ref[rows, :] = _dot(ltri, mb.astype(BF16)) + carry
        carry = carry + jnp.sum(mb, axis=0, keepdims=True)


def _select_kernel(aff_ref, idx_ref, gate_ref, bits_ref, mask_ref, pre_ref, key_ref,
                   *, n_experts, cap):
    s = aff_ref.shape[1]
    bits_ref[...] = pltpu.bitcast(aff_ref[0], jnp.int32)

    def search(i, tau):
        cand = tau | jnp.left_shift(jnp.int32(1), 30 - i)
        cnt = jnp.sum((bits_ref[...] >= cand).astype(F32), axis=0, keepdims=True)
        return jnp.where(cnt >= cap, cand, tau)
    tau = lax.fori_loop(0, 31, search, jnp.zeros((1, LANES), jnp.int32))

    r = lax.broadcasted_iota(jnp.int32, (PREFIX_BLK, PREFIX_BLK), 0)
    c = lax.broadcasted_iota(jnp.int32, (PREFIX_BLK, PREFIX_BLK), 1)
    ltri = (c < r).astype(BF16)

    gt = bits_ref[...] > tau
    n_gt = jnp.sum(gt.astype(F32), axis=0, keepdims=True)
    mask_ref[...] = (bits_ref[...] == tau).astype(F32)
    _prefix_rows_into(mask_ref, pre_ref, ltri)
    sel = gt | ((mask_ref[...] > 0.5) & (pre_ref[...] < cap - n_gt))
    mask_ref[...] = sel.astype(F32)
    _prefix_rows_into(mask_ref, pre_ref, ltri)
    key_ref[...] = jnp.where(mask_ref[...] > 0.5, pre_ref[...], -1.0)

    slot_iota = lax.broadcasted_iota(jnp.int32, (SEL_CHUNK, cap), 1).astype(F32)
    row_iota = lax.broadcasted_iota(jnp.int32, (SEL_CHUNK, cap), 0).astype(F32)
    base = pl.program_id(0) * s
    for e in range(n_experts):
        def body(ch, carry, e=e):
            idx_acc, gate_acc = carry
            rows = pl.ds(ch * SEL_CHUNK, SEL_CHUNK)
            hit = key_ref[rows, e:e + 1] == slot_iota
            tok = row_iota + jnp.asarray(base + ch * SEL_CHUNK, F32)
            idx_c = jnp.where(hit, tok, 0.0)
            gate_c = jnp.where(hit, aff_ref[0, rows, e:e + 1], 0.0)
            idx_acc = idx_acc + idx_c.reshape(SEL_CHUNK // 8, 8, cap).sum(axis=0)
            gate_acc = gate_acc + gate_c.reshape(SEL_CHUNK // 8, 8, cap).sum(axis=0)
            return idx_acc, gate_acc
        zero = jnp.zeros((8, cap), F32)
        idx_acc, gate_acc = lax.fori_loop(0, s // SEL_CHUNK, body, (zero, zero))
        idx_ref[0, e:e + 1, :] = jnp.sum(idx_acc, axis=0, keepdims=True).astype(jnp.int32)
        gate_ref[0, e:e + 1, :] = jnp.sum(gate_acc, axis=0, keepdims=True)


def _select(aff3, n_experts, cap):
    b, s, _ = aff3.shape
    return pl.pallas_call(
        functools.partial(_select_kernel, n_experts=n_experts, cap=cap),
        grid=(b,),
        in_specs=[pl.BlockSpec((1, s, LANES), lambda i: (i, 0, 0))],
        out_specs=[pl.BlockSpec((1, n_experts, cap), lambda i: (i, 0, 0))] * 2,
        out_shape=[jax.ShapeDtypeStruct((b, n_experts, cap), jnp.int32),
                   jax.ShapeDtypeStruct((b, n_experts, cap), F32)],
        scratch_shapes=[pltpu.VMEM((s, LANES), jnp.int32), pltpu.VMEM((s, LANES), F32),
                        pltpu.VMEM((s, LANES), F32), pltpu.VMEM((s, LANES), F32)],
        compiler_params=_params(("arbitrary",)),
        name="moe_select",
    )(aff3)


DOWN_COLS = 512
ROW_LOOKAHEAD = 2


def _moe_kernel(idx_ref, h_hbm, x_hbm, gate_ref, gprev_ref, wg_ref, wu_ref, wd_ref, o_hbm,
                xe32_ref, xe16_ref, y_ref, row_ref, sem_ref, bar_ref, *, rows, n_experts, n_f):
    del x_hbm
    e = pl.program_id(0)
    f = pl.program_id(1)
    ch = rows // n_f
    gather_sem, scatter_sem = sem_ref.at[0], sem_ref.at[1]
    acc_sem = lambda c: sem_ref.at[2 + c % 2]
    slot = e % 2
    pe = jnp.maximum(e - 1, 0)
    valid = jnp.where(e > 0, 1.0, 0.0).astype(F32)

    def h_row_in(t, i):
        return pltpu.make_async_copy(h_hbm.at[pl.ds(t, 1)], xe32_ref.at[pl.ds(i, 1)], gather_sem)

    def acc_row_in(c):
        return lambda t, i: pltpu.make_async_copy(o_hbm.at[pl.ds(t, 1)], row_ref.at[c, pl.ds(i, 1)],
                                                  acc_sem(c))

    def acc_row_out(c):
        return lambda t, i: pltpu.make_async_copy(row_ref.at[c, pl.ds(i, 1)], o_hbm.at[pl.ds(t, 1)],
                                                  scatter_sem)

    def start_unrolled(list_base, n, make, dst_base=0):
        for i in range(n):
            make(idx_ref[list_base + i], dst_base + i).start()

    def start_looped(list_base, n, make):
        def body(i, _):
            make(idx_ref[list_base + i], i).start()
            return 0
        lax.fori_loop(0, n, body, 0)

    def wait_rows(n_rows, sem):
        pltpu.make_async_copy(h_hbm.at[pl.ds(0, n_rows)], xe32_ref.at[pl.ds(0, n_rows)], sem).wait()

    def gated_rows(c, y_slot, g_ref, scale):
        crows = pl.ds(c * ch, ch)
        return row_ref[c] + y_ref[y_slot, crows, :] * (g_ref[0, crows, 0:1] * scale)

    @pl.when((e == 0) & (f == 0))
    def _():
        start_looped(0, rows, h_row_in)
        y_ref[1] = jnp.zeros(y_ref.shape[1:], F32)
        for c in range(ROW_LOOKAHEAD):
            start_looped(c * ch, ch, acc_row_in(c))

    def step(f):
        first, last = f == 0, f == n_f - 1
        wait_rows(ch, acc_sem(f))
        row_ref[f] = gated_rows(f, 1 - slot, gprev_ref, valid)
        pl.semaphore_signal(bar_ref, 1)
        pl.semaphore_wait(bar_ref, 1)
        if first:
            wait_rows(rows, gather_sem)
            xe16_ref[...] = xe32_ref[...].astype(BF16)
        start_unrolled(pe * rows + f * ch, ch, acc_row_out(f))

        xe = xe16_ref[...]
        g = _dot(xe, wg_ref[0].astype(BF16))
        start_unrolled(jnp.minimum(e + 1, n_experts - 1) * rows + f * ch, ch, h_row_in, f * ch)
        u = _dot(xe, wu_ref[0].astype(BF16))
        act = (g * (1.0 / (1.0 + jnp.exp(-g))) * u).astype(BF16)
        for n0 in range(0, y_ref.shape[2], DOWN_COLS):
            cols = slice(n0, n0 + DOWN_COLS)
            part = _dot(act, wd_ref[0, :, cols].astype(BF16))
            if first:
                y_ref[slot, :, cols] = part
            else:
                y_ref[slot, :, cols] += part

        if last:
            wait_rows(rows, scatter_sem)
            for c in range(ROW_LOOKAHEAD):
                start_unrolled(e * rows + c * ch, ch, acc_row_in(c))
        elif f + ROW_LOOKAHEAD < n_f:
            start_unrolled(pe * rows + (f + ROW_LOOKAHEAD) * ch, ch, acc_row_in(f + ROW_LOOKAHEAD))

    for fs in range(n_f):
        pl.when(f == fs)(functools.partial(step, fs))

    @pl.when((e == n_experts - 1) & (f == n_f - 1))
    def _():
        wait_rows(rows, gather_sem)
        for c in range(n_f):
            wait_rows(ch, acc_sem(c))
            if c + ROW_LOOKAHEAD < n_f:
                start_looped(e * rows + (c + ROW_LOOKAHEAD) * ch, ch, acc_row_in(c + ROW_LOOKAHEAD))
            row_ref[c] = gated_rows(c, slot, gate_ref, 1.0)
            start_looped(e * rows + c * ch, ch, acc_row_out(c))
        wait_rows(rows, scatter_sem)


def _moe_experts(idx_flat, gate_rows, h, x, w_gate, w_up, w_down, tf):
    t, d = x.shape
    ne, _, dff = w_gate.shape
    rows = idx_flat.shape[0] // ne
    n_f = dff // tf
    assert n_f >= 2
    grid_spec = pltpu.PrefetchScalarGridSpec(
        num_scalar_prefetch=1,
        grid=(ne, n_f),
        in_specs=[pl.BlockSpec(memory_space=pl.ANY),
                  pl.BlockSpec(memory_space=pl.ANY),
                  pl.BlockSpec((1, rows, LANES), lambda e, f, idx: (e, 0, 0)),
                  pl.BlockSpec((1, rows, LANES), lambda e, f, idx: (jnp.maximum(e - 1, 0), 0, 0)),
                  pl.BlockSpec((1, d, tf), lambda e, f, idx: (e, 0, f)),
                  pl.BlockSpec((1, d, tf), lambda e, f, idx: (e, 0, f)),
                  pl.BlockSpec((1, tf, d), lambda e, f, idx: (e, f, 0))],
        out_specs=pl.BlockSpec(memory_space=pl.ANY),
        scratch_shapes=[pltpu.VMEM((rows, d), F32), pltpu.VMEM((rows, d), BF16),
                        pltpu.VMEM((2, rows, d), F32), pltpu.VMEM((n_f, rows // n_f, d), F32),
                        pltpu.SemaphoreType.DMA((4,)), pltpu.SemaphoreType.REGULAR],
    )
    return pl.pallas_call(
        functools.partial(_moe_kernel, rows=rows, n_experts=ne, n_f=n_f),
        grid_spec=grid_spec,
        out_shape=jax.ShapeDtypeStruct((t, d), F32),
        input_output_aliases={2: 0},
        compiler_params=pltpu.CompilerParams(dimension_semantics=("arbitrary", "arbitrary"),
                                             vmem_limit_bytes=MOE_VMEM_LIMIT),
        name="moe_experts",
    )(idx_flat, h, x, gate_rows, gate_rows, w_gate, w_up, w_down)


def _moe(x, s, g_moe, w_router, w_gate, w_up, w_down):
    t, d = x.shape
    b = t // s
    ne = w_router.shape[1]
    cap = EC_CAPACITY * s // ne
    h, aff = _router(x, g_moe, w_router, tm=512)
    idx, gate = _select(aff.reshape(b, s, LANES), ne, cap)
    idx_flat = idx.transpose(1, 0, 2).reshape(ne * b * cap)
    gate_rows = jnp.broadcast_to(gate.transpose(1, 0, 2).reshape(ne, b * cap, 1), (ne, b * cap, LANES))
    return _moe_experts(idx_flat, gate_rows, h, x, w_gate, w_up, w_down, tf=256)


def kernel(x, mem, positions, g_mix, w_in, g_qa, g_ka, g_qb, g_kb, sink_b, g_oa, g_ob, w_out,
           g_cross, g_mem, w_q_mem, w_kv_mem, g_qm, g_km, w_o_mem, g_moe, w_router, w_gate,
           w_up, w_down):
    b, s, d = x.shape
    t = b * s
    depth = g_mix.shape[0]
    width_a = N_HEADS_A * HEAD_DIM
    width_b = N_HEADS_B * HEAD_DIM
    kv_b = width_b // GQA_REP
    col = [0, width_a, 2 * width_a, 3 * width_a, 3 * width_a + width_b, 3 * width_a + width_b + kv_b]
    col = [c // HEAD_DIM for c in col]

    cos2, sin2 = _rope_tables(positions)
    xf = x.reshape(t, d)
    for l in range(depth):
        proj = _norm_matmul(xf, g_mix[l], w_in[l], tm=1024, tn=512)
        proj3 = proj.reshape(b, s, proj.shape[1])
        oa = _attn_a(proj3, cos2, sin2, g_qa[l], g_ka[l], col[0], col[1], col[2])
        ob = _attn_b(proj3, cos2, sin2, g_qb[l], g_kb[l], sink_b[l], col[3], col[4], col[5])
        xf = _out_proj(oa.reshape(t, width_a), ob.reshape(t, width_b), g_oa[l], g_ob[l], w_out[l],
                       xf, tm=1024, tn=512)
        kmem, vmem = _mem_kv(mem, g_mem[l], w_kv_mem[l], g_km[l])
        xf = _cross_attn(xf, g_cross[l], w_q_mem[l], g_qm[l], kmem, vmem, w_o_mem[l], s, tm=512)
        xf = _moe(xf, s, g_moe[l], w_router[l], w_gate[l], w_up[l], w_down[l])
    return xf.reshape(b, s, d)
```

```python
import functools

import jax
import jax.numpy as jnp
from jax import lax
from jax.experimental import pallas as pl
from jax.experimental.pallas import tpu as pltpu

F32 = jnp.float32
BF16 = jnp.bfloat16

HEAD_DIM = 128
LANES = 128
EPS = 1e-6
NEG_INF = -1e30
ROPE_THETA = 10000.0
DILATED_GROUPS = ((128, 1), (512, 4), (2048, 16))
HALF_WINDOW_B = 128
GQA_REP = 4
N_HEADS_A = 8
N_HEADS_B = 8
N_HEADS_MEM = 4
N_EXPERTS = 16
EC_CAPACITY = 2
VMEM_LIMIT = 56 * 1024 * 1024
MOE_VMEM_LIMIT = 60 * 1024 * 1024


def _params(sem):
    return pltpu.CompilerParams(dimension_semantics=sem, vmem_limit_bytes=VMEM_LIMIT)


def _rms(x, g):
    return x * lax.rsqrt(jnp.mean(x * x, axis=-1, keepdims=True) + EPS) * g


def _dot(a, b):
    return jnp.dot(a, b, preferred_element_type=F32)


def _dot_nt(a, b):
    return lax.dot_general(a, b, (((1,), (1,)), ((), ())), preferred_element_type=F32)


def _rope_kernel(pos_ref, invf_ref, cos_ref, sin_ref):
    ang = pos_ref[...] * invf_ref[...]
    lane = lax.broadcasted_iota(jnp.int32, ang.shape, 1)
    cos_ref[...] = jnp.cos(ang)
    sin_ref[...] = jnp.where(lane < HEAD_DIM // 2, -jnp.sin(ang), jnp.sin(ang))


def _rope_tables(positions):
    b, s = positions.shape
    t = b * s
    inv_freq = ROPE_THETA ** (-jnp.arange(0, HEAD_DIM, 2, dtype=F32) / HEAD_DIM)
    invf2 = jnp.concatenate([inv_freq, inv_freq]).reshape(1, HEAD_DIM)
    pos = jnp.broadcast_to(positions.reshape(t, 1).astype(F32), (t, HEAD_DIM))
    ts = 1024
    cos2, sin2 = pl.pallas_call(
        _rope_kernel,
        grid=(t // ts,),
        in_specs=[pl.BlockSpec((ts, HEAD_DIM), lambda i: (i, 0)),
                  pl.BlockSpec((1, HEAD_DIM), lambda i: (0, 0))],
        out_specs=[pl.BlockSpec((ts, HEAD_DIM), lambda i: (i, 0))] * 2,
        out_shape=[jax.ShapeDtypeStruct((t, HEAD_DIM), F32)] * 2,
        compiler_params=_params(("arbitrary",)),
        name="rope_tables",
    )(pos, invf2)
    return cos2.reshape(b, s, HEAD_DIM), sin2.reshape(b, s, HEAD_DIM)


ROW_CHUNK = 256
ROPE_CHUNK = 1024


STAGE_ROWS = 512
DOT_ROWS = 1024


def _stream_rows(srcs, stage_refs, sem_ref, n_rows, consume):
    chunk = stage_refs[0].shape[1]
    n_chunks = n_rows // chunk

    def copies(c, slot):
        return [pltpu.make_async_copy(src.at[pl.ds(c * chunk, chunk)], stage.at[slot],
                                      sem_ref.at[i, slot])
                for i, (src, stage) in enumerate(zip(srcs, stage_refs))]

    for cp in copies(0, 0):
        cp.start()

    def body(c, _):
        slot = c % 2

        @pl.when(c + 1 < n_chunks)
        def _():
            for cp in copies(c + 1, 1 - slot):
                cp.start()
        for cp in copies(c, slot):
            cp.wait()
        consume(pl.ds(c * chunk, chunk), [stage[slot] for stage in stage_refs])
        return 0
    lax.fori_loop(0, n_chunks, body, 0)


def _norm_matmul_kernel(x_hbm, g_ref, w_ref, o_ref, xn_ref, w16_ref, stage_ref, sem_ref):
    t = xn_ref.shape[0]

    @pl.when(pl.program_id(0) == 0)
    def _():
        def consume(rows, vals):
            xn_ref[rows, :] = _rms(vals[0], g_ref[...]).astype(BF16)
        _stream_rows([x_hbm], [stage_ref], sem_ref, t, consume)

    w16_ref[...] = w_ref[...].astype(BF16)

    def body(rb, _):
        rows = pl.ds(rb * DOT_ROWS, DOT_ROWS)
        o_ref[rows, :] = _dot(xn_ref[rows, :], w16_ref[...]).astype(o_ref.dtype)
        return 0
    lax.fori_loop(0, t // DOT_ROWS, body, 0)


def _norm_matmul(x, g, w, tn):
    t, k = x.shape
    n = w.shape[1]
    return pl.pallas_call(
        _norm_matmul_kernel,
        grid=(n // tn,),
        in_specs=[pl.BlockSpec(memory_space=pl.ANY),
                  pl.BlockSpec((1, k), lambda j: (0, 0)),
                  pl.BlockSpec((k, tn), lambda j: (0, j))],
        out_specs=pl.BlockSpec((t, tn), lambda j: (0, j)),
        out_shape=jax.ShapeDtypeStruct((t, n), BF16),
        scratch_shapes=[pltpu.VMEM((t, k), BF16), pltpu.VMEM((k, tn), BF16),
                        pltpu.VMEM((2, STAGE_ROWS, k), F32), pltpu.SemaphoreType.DMA((1, 2))],
        compiler_params=_params(("arbitrary",)),
        name="norm_in_proj",
    )(x, g.reshape(1, k), w)


def _norm_rope_into(src_ref, g_ref, cos_ref, sin_ref, dst_ref, scale):
    def body(c, _):
        rows = pl.ds(c * ROPE_CHUNK, ROPE_CHUNK)
        t = _rms(src_ref[0, rows, :].astype(F32), g_ref[...])
        t = t * cos_ref[0, rows, :] + pltpu.roll(t, HEAD_DIM // 2, axis=1) * sin_ref[0, rows, :]
        dst_ref[rows, :] = t * scale
        return 0
    lax.fori_loop(0, dst_ref.shape[0] // ROPE_CHUNK, body, 0)


def _band_mask(q0, k0, nq, nk, half):
    qpos = q0 + lax.broadcasted_iota(jnp.int32, (nq, nk), 0)
    kpos = k0 + lax.broadcasted_iota(jnp.int32, (nq, nk), 1)
    return jnp.abs(qpos - kpos) <= half


QBLK_A = 128
KWIN_A = 256
QROWS_PER_ITER_A = 512


def _rows(start, size, stride):
    return pl.ds(start, size) if stride == 1 else pl.ds(start, size, stride=stride)


def _attn_a_kernel(q_ref, k_ref, v_ref, cos_ref, sin_ref, gq_ref, gk_ref, o_ref,
                   qn_ref, kn_ref, vf_ref, acc_ref, ml_ref):
    s = qn_ref.shape[0]
    _norm_rope_into(q_ref, gq_ref, cos_ref, sin_ref, qn_ref, 1.0 / (HEAD_DIM ** 0.5))
    _norm_rope_into(k_ref, gk_ref, cos_ref, sin_ref, kn_ref, 1.0)

    def widen(c, _):
        rows = pl.ds(c * ROPE_CHUNK, ROPE_CHUNK)
        vf_ref[rows, :] = v_ref[0, rows, :].astype(F32)
        return 0
    lax.fori_loop(0, s // ROPE_CHUNK, widen, 0)

    groups = sorted(DILATED_GROUPS, key=lambda wd: -wd[1])
    for gi, (w, d) in enumerate(groups):
        half = w // (2 * d)
        ln = s // d
        whole = ln <= KWIN_A
        qblk = ln if whole else QBLK_A
        kwin = ln if whole else KWIN_A
        nblk = ln // qblk
        per_iter = QROWS_PER_ITER_A // qblk

        def body(it, _, d=d, half=half, ln=ln, nblk=nblk, qblk=qblk, kwin=kwin, per_iter=per_iter,
                 first=(gi == 0)):
            blocks = []
            for j in range(per_iter):
                n = it * per_iter + j
                r = n // nblk
                a = (n % nblk) * qblk
                k0 = jnp.clip(a - (kwin - qblk) // 2, 0, ln - kwin)
                qrows = _rows(r + d * a, qblk, d)
                krows = _rows(r + d * k0, kwin, d)
                q = qn_ref[qrows, :].astype(BF16)
                k = kn_ref[krows, :].astype(BF16)
                v = vf_ref[krows, :].astype(BF16)
                old = None if first else (ml_ref[qrows, :], acc_ref[qrows, :])
                blocks.append((a, k0, qrows, q, k, v, old))
            scores = [_dot_nt(q, k) for _, _, _, q, k, _, _ in blocks]
            probs = []
            for (a, k0, _, _, _, _, old), sc in zip(blocks, scores):
                sc = jnp.where(_band_mask(a, k0, qblk, kwin, half), sc, NEG_INF)
                m_new = jnp.max(sc, axis=-1, keepdims=True)
                if old is not None:
                    m_new = jnp.maximum(old[0][:, 0:1], m_new)
                p = jnp.exp(sc - m_new)
                probs.append((m_new, jnp.sum(p, axis=-1, keepdims=True), p.astype(BF16)))
            results = []
            for (_, _, qrows, _, _, v, old), (m_new, l_new, p) in zip(blocks, probs):
                acc_new = _dot(p, v)
                if old is not None:
                    alpha = jnp.exp(old[0][:, 0:1] - m_new)
                    l_new = alpha * old[0][:, 1:2] + l_new
                    acc_new = alpha * old[1] + acc_new
                results.append((qrows, m_new, l_new, acc_new))
            lane = lax.broadcasted_iota(jnp.int32, (qblk, ml_ref.shape[1]), 1)
            for qrows, m_new, l_new, acc_new in results:
                ml_ref[qrows, :] = jnp.where(lane == 0, m_new, l_new)
                acc_ref[qrows, :] = acc_new
            return 0

        lax.fori_loop(0, d * nblk // per_iter, body, 0)

    def fin(c, _):
        rows = pl.ds(c * ROW_CHUNK, ROW_CHUNK)
        o_ref[0, rows, :] = acc_ref[rows, :] / ml_ref[rows, 1:2]
        return 0
    lax.fori_loop(0, s // ROW_CHUNK, fin, 0)


def _attn_a(proj3, cos2, sin2, g_q, g_k, col_q, col_k, col_v):
    b, s, _ = proj3.shape
    nh = N_HEADS_A
    blk = lambda col: pl.BlockSpec((1, s, HEAD_DIM), lambda i, h, col=col: (i, 0, col + h))
    tab = pl.BlockSpec((1, s, HEAD_DIM), lambda i, h: (i, 0, 0))
    gain = pl.BlockSpec((1, HEAD_DIM), lambda i, h: (0, 0))
    return pl.pallas_call(
        _attn_a_kernel,
        grid=(b, nh),
        in_specs=[blk(col_q), blk(col_k), blk(col_v), tab, tab, gain, gain],
        out_specs=pl.BlockSpec((1, s, HEAD_DIM), lambda i, h: (i, 0, h)),
        out_shape=jax.ShapeDtypeStruct((b, s, nh * HEAD_DIM), F32),
        scratch_shapes=[pltpu.VMEM((s, HEAD_DIM), F32), pltpu.VMEM((s, HEAD_DIM), F32),
                        pltpu.VMEM((s, HEAD_DIM), F32), pltpu.VMEM((s, HEAD_DIM), F32),
                        pltpu.VMEM((s, LANES), F32)],
        compiler_params=_params(("arbitrary", "arbitrary")),
        name="attn_dilated",
    )(proj3, proj3, proj3, cos2, sin2, g_q.reshape(1, HEAD_DIM), g_k.reshape(1, HEAD_DIM))


QBLK_B = 256
KWIN_B = QBLK_B + 2 * HALF_WINDOW_B
BLOCKS_PER_ITER_B = 2


def _attn_b_kernel(sink_ref, q_ref, k_ref, v_ref, cos_ref, sin_ref, gq_ref, gk_ref, o_ref,
                   qn_ref, kn_ref):
    s = qn_ref.shape[0]
    sink = sink_ref[pl.program_id(1)]
    _norm_rope_into(q_ref, gq_ref, cos_ref, sin_ref, qn_ref, 1.0 / (HEAD_DIM ** 0.5))
    _norm_rope_into(k_ref, gk_ref, cos_ref, sin_ref, kn_ref, 1.0)

    def body(i, _):
        blocks = []
        for j in range(BLOCKS_PER_ITER_B):
            a = (i * BLOCKS_PER_ITER_B + j) * QBLK_B
            k0 = pl.multiple_of(jnp.clip(a - HALF_WINDOW_B, 0, s - KWIN_B), HALF_WINDOW_B)
            q = qn_ref[pl.ds(a, QBLK_B), :].astype(BF16)
            k = kn_ref[pl.ds(k0, KWIN_B), :].astype(BF16)
            v = v_ref[0, pl.ds(k0, KWIN_B), :]
            blocks.append((a, k0, v, _dot_nt(q, k)))
        probs = []
        for a, k0, v, sc in blocks:
            sc = jnp.where(_band_mask(a, k0, QBLK_B, KWIN_B, HALF_WINDOW_B), sc, NEG_INF)
            m = jnp.maximum(jnp.max(sc, axis=-1, keepdims=True), sink)
            p = jnp.exp(sc - m)
            l = jnp.sum(p, axis=-1, keepdims=True) + jnp.exp(sink - m)
            probs.append((a, v, p.astype(BF16), l))
        for a, v, p, l in probs:
            o_ref[0, pl.ds(a, QBLK_B), :] = _dot(p, v) / l
        return 0
    lax.fori_loop(0, s // (QBLK_B * BLOCKS_PER_ITER_B), body, 0)


def _attn_b(proj3, cos2, sin2, g_q, g_k, sink, col_q, col_k, col_v):
    b, s, _ = proj3.shape
    nh = N_HEADS_B
    qblk = pl.BlockSpec((1, s, HEAD_DIM), lambda i, h: (i, 0, col_q + h))
    kblk = pl.BlockSpec((1, s, HEAD_DIM), lambda i, h: (i, 0, col_k + h // GQA_REP))
    vblk = pl.BlockSpec((1, s, HEAD_DIM), lambda i, h: (i, 0, col_v + h // GQA_REP))
    tab = pl.BlockSpec((1, s, HEAD_DIM), lambda i, h: (i, 0, 0))
    gain = pl.BlockSpec((1, HEAD_DIM), lambda i, h: (0, 0))
    return pl.pallas_call(
        _attn_b_kernel,
        grid=(b, nh),
        in_specs=[pl.BlockSpec(memory_space=pltpu.SMEM), qblk, kblk, vblk, tab, tab, gain, gain],
        out_specs=pl.BlockSpec((1, s, HEAD_DIM), lambda i, h: (i, 0, h)),
        out_shape=jax.ShapeDtypeStruct((b, s, nh * HEAD_DIM), F32),
        scratch_shapes=[pltpu.VMEM((s, HEAD_DIM), F32), pltpu.VMEM((s, HEAD_DIM), F32)],
        compiler_params=_params(("arbitrary", "arbitrary")),
        name="attn_window_gqa",
    )(sink, proj3, proj3, proj3, cos2, sin2, g_q.reshape(1, HEAD_DIM), g_k.reshape(1, HEAD_DIM))


def _out_proj_kernel(oa_hbm, ob_hbm, ga_ref, gb_ref, w_ref, x_ref, o_ref,
                     mix_ref, w16_ref, sa_ref, sb_ref, sem_ref):
    j, i = pl.program_id(0), pl.program_id(1)
    wa = sa_ref.shape[2]
    tm = o_ref.shape[0]

    @pl.when((j == 0) & (i == 0))
    def _():
        def consume(rows, vals):
            mix_ref[rows, :wa] = _rms(vals[0], ga_ref[...]).astype(BF16)
            mix_ref[rows, wa:] = _rms(vals[1], gb_ref[...]).astype(BF16)
        _stream_rows([oa_hbm, ob_hbm], [sa_ref, sb_ref], sem_ref, mix_ref.shape[0], consume)

    @pl.when(i == 0)
    def _():
        w16_ref[...] = w_ref[...].astype(BF16)

    o_ref[...] = x_ref[...] + _dot(mix_ref[pl.ds(i * tm, tm), :], w16_ref[...])


def _out_proj(oa, ob, g_oa, g_ob, w_out, x, tm, tn):
    t, wa = oa.shape
    wb = ob.shape[1]
    k, n = w_out.shape
    return pl.pallas_call(
        _out_proj_kernel,
        grid=(n // tn, t // tm),
        in_specs=[pl.BlockSpec(memory_space=pl.ANY),
                  pl.BlockSpec(memory_space=pl.ANY),
                  pl.BlockSpec((1, wa), lambda j, i: (0, 0)),
                  pl.BlockSpec((1, wb), lambda j, i: (0, 0)),
                  pl.BlockSpec((k, tn), lambda j, i: (0, j)),
                  pl.BlockSpec((tm, tn), lambda j, i: (i, j))],
        out_specs=pl.BlockSpec((tm, tn), lambda j, i: (i, j)),
        out_shape=jax.ShapeDtypeStruct((t, n), F32),
        scratch_shapes=[pltpu.VMEM((t, k), BF16), pltpu.VMEM((k, tn), BF16),
                        pltpu.VMEM((2, STAGE_ROWS // 2, wa), F32),
                        pltpu.VMEM((2, STAGE_ROWS // 2, wb), F32),
                        pltpu.SemaphoreType.DMA((2, 2))],
        compiler_params=_params(("arbitrary", "arbitrary")),
        name="mix_out_proj",
    )(oa, ob, g_oa.reshape(1, wa), g_ob.reshape(1, wb), w_out, x)


def _mem_kv_kernel(m_ref, g_ref, w_ref, gk_ref, k_ref, v_ref):
    wm = k_ref.shape[2]
    mn = _rms(m_ref[0], g_ref[...]).astype(BF16)
    kv = _dot(mn, w_ref[...].astype(BF16))
    for h in range(wm // HEAD_DIM):
        cols = slice(h * HEAD_DIM, (h + 1) * HEAD_DIM)
        k_ref[0, :, cols] = _rms(kv[:, cols], gk_ref[...]).astype(BF16)
    v_ref[0] = kv[:, wm:].astype(BF16)


def _mem_kv(mem, g_mem, w_kv, g_km):
    b, nm, d = mem.shape
    wm = w_kv.shape[1] // 2
    return pl.pallas_call(
        _mem_kv_kernel,
        grid=(b,),
        in_specs=[pl.BlockSpec((1, nm, d), lambda i: (i, 0, 0)),
                  pl.BlockSpec((1, d), lambda i: (0, 0)),
                  pl.BlockSpec((d, 2 * wm), lambda i: (0, 0)),
                  pl.BlockSpec((1, HEAD_DIM), lambda i: (0, 0))],
        out_specs=[pl.BlockSpec((1, nm, wm), lambda i: (i, 0, 0))] * 2,
        out_shape=[jax.ShapeDtypeStruct((b, nm, wm), BF16)] * 2,
        compiler_params=_params(("arbitrary",)),
        name="mem_kv_proj",
    )(mem, g_mem.reshape(1, d), w_kv, g_km.reshape(1, HEAD_DIM))


def _cross_kernel(x_ref, g_ref, wq_ref, gq_ref, k_ref, v_ref, wo_ref, o_ref, wq16_ref, wo16_ref):
    @pl.when(pl.program_id(0) == 0)
    def _():
        wq16_ref[...] = wq_ref[...].astype(BF16)
        wo16_ref[...] = wo_ref[...].astype(BF16)

    x = x_ref[...]
    hn = _rms(x, g_ref[...]).astype(BF16)
    q = _dot(hn, wq16_ref[...])
    outs = []
    for h in range(q.shape[1] // HEAD_DIM):
        cols = slice(h * HEAD_DIM, (h + 1) * HEAD_DIM)
        qh = (_rms(q[:, cols], gq_ref[...]) * (1.0 / (HEAD_DIM ** 0.5))).astype(BF16)
        sc = _dot_nt(qh, k_ref[0, :, cols])
        m = jnp.max(sc, axis=-1, keepdims=True)
        p = jnp.exp(sc - m)
        l = jnp.sum(p, axis=-1, keepdims=True)
        outs.append((_dot(p.astype(BF16), v_ref[0, :, cols]) / l).astype(BF16))
    o = jnp.concatenate(outs, axis=-1)
    o_ref[...] = x + _dot(o, wo16_ref[...])


def _cross_attn(x, g_cross, w_q, g_qm, kmem, vmem, w_o, s, tm):
    t, d = x.shape
    wm = w_q.shape[1]
    nm = kmem.shape[1]
    per_b = s // tm
    return pl.pallas_call(
        _cross_kernel,
        grid=(t // tm,),
        in_specs=[pl.BlockSpec((tm, d), lambda i: (i, 0)),
                  pl.BlockSpec((1, d), lambda i: (0, 0)),
                  pl.BlockSpec((d, wm), lambda i: (0, 0)),
                  pl.BlockSpec((1, HEAD_DIM), lambda i: (0, 0)),
                  pl.BlockSpec((1, nm, wm), lambda i: (i // per_b, 0, 0)),
                  pl.BlockSpec((1, nm, wm), lambda i: (i // per_b, 0, 0)),
                  pl.BlockSpec((wm, d), lambda i: (0, 0))],
        out_specs=pl.BlockSpec((tm, d), lambda i: (i, 0)),
        out_shape=jax.ShapeDtypeStruct((t, d), F32),
        scratch_shapes=[pltpu.VMEM((d, wm), BF16), pltpu.VMEM((wm, d), BF16)],
        compiler_params=_params(("arbitrary",)),
        name="mem_cross_attn",
    )(x, g_cross.reshape(1, d), w_q, g_qm.reshape(1, HEAD_DIM), kmem, vmem, w_o)


def _router_kernel(x_ref, g_ref, wr_ref, h_ref, aff_ref, afft_ref, *, n_experts):
    h = _rms(x_ref[...], g_ref[...])
    h_ref[...] = h
    logits = jnp.dot(h, wr_ref[...], preferred_element_type=F32, precision=lax.Precision.HIGHEST)
    lane = lax.broadcasted_iota(jnp.int32, logits.shape, 1)
    logits = jnp.where(lane < n_experts, logits, NEG_INF)
    m = jnp.max(logits, axis=-1, keepdims=True)
    e = jnp.exp(logits - m)
    aff = e / jnp.sum(e, axis=-1, keepdims=True)
    aff_ref[...] = aff
    afft_ref[0] = jnp.transpose(aff)[:n_experts, :]


def _router(x, s, g_moe, w_router, tm):
    t, d = x.shape
    ne = w_router.shape[1]
    per_b = s // tm
    wr = jnp.pad(w_router, ((0, 0), (0, LANES - ne)))
    return pl.pallas_call(
        functools.partial(_router_kernel, n_experts=ne),
        grid=(t // tm,),
        in_specs=[pl.BlockSpec((tm, d), lambda i: (i, 0)),
                  pl.BlockSpec((1, d), lambda i: (0, 0)),
                  pl.BlockSpec((d, LANES), lambda i: (0, 0))],
        out_specs=[pl.BlockSpec((tm, d), lambda i: (i, 0)),
                   pl.BlockSpec((tm, LANES), lambda i: (i, 0)),
                   pl.BlockSpec((1, ne, tm), lambda i: (i // per_b, 0, i % per_b))],
        out_shape=[jax.ShapeDtypeStruct((t, d), F32), jax.ShapeDtypeStruct((t, LANES), F32),
                   jax.ShapeDtypeStruct((t // s, ne, s), F32)],
        compiler_params=_params(("arbitrary",)),
        name="moe_router",
    )(x, g_moe.reshape(1, d), wr)


PREFIX_BLK = 256
SEL_BLK = 128


def _prefix_rows_into(mask_ref, out_ref, ltri):
    carry = jnp.zeros((1, mask_ref.shape[1]), F32)
    for blk in range(mask_ref.shape[0] // PREFIX_BLK):
        rows = pl.ds(blk * PREFIX_BLK, PREFIX_BLK)
        mb = mask_ref[rows, :]
        out_ref[rows, :] = _dot(ltri, mb.astype(BF16)) + carry
        carry = carry + jnp.sum(mb, axis=0, keepdims=True)


def _select_kernel(aff_ref, afft_ref, idx_ref, gate_ref, mask_ref, pre_ref, key_ref,
                   start_ref, start_smem, acc_ref, sem_ref, *, n_experts, cap):
    s = aff_ref.shape[1]
    n_blk = s // SEL_BLK
    n_col = cap // LANES

    def search(i, tau):
        cand = tau | jnp.left_shift(jnp.int32(1), 30 - i)
        cnt = jnp.sum((afft_ref[0] >= pltpu.bitcast(cand, F32)).astype(F32), axis=1, keepdims=True)
        return jnp.where(cnt >= cap, cand, tau)
    tau = lax.fori_loop(0, 31, search, jnp.zeros((n_experts, 1), jnp.int32))
    eye = (lax.broadcasted_iota(jnp.int32, (n_experts, LANES), 0)
           == lax.broadcasted_iota(jnp.int32, (n_experts, LANES), 1))
    tau_row = jnp.sum(jnp.where(eye, pltpu.bitcast(tau, F32), 0.0), axis=0, keepdims=True)

    r = lax.broadcasted_iota(jnp.int32, (PREFIX_BLK, PREFIX_BLK), 0)
    c = lax.broadcasted_iota(jnp.int32, (PREFIX_BLK, PREFIX_BLK), 1)
    ltri = (c < r).astype(BF16)

    gt = aff_ref[0] > tau_row
    n_gt = jnp.sum(gt.astype(F32), axis=0, keepdims=True)
    mask_ref[...] = (aff_ref[0] == tau_row).astype(F32)
    _prefix_rows_into(mask_ref, pre_ref, ltri)
    sel = gt | ((mask_ref[...] > 0.5) & (pre_ref[...] < cap - n_gt))
    mask_ref[...] = sel.astype(F32)
    _prefix_rows_into(mask_ref, pre_ref, ltri)
    key_ref[...] = jnp.where(mask_ref[...] > 0.5, pre_ref[...], -1.0)

    start_ref[...] = pre_ref[pl.ds(0, n_blk, stride=SEL_BLK), :].astype(jnp.int32)
    to_smem = pltpu.make_async_copy(start_ref, start_smem, sem_ref.at[0])
    to_smem.start()
    acc_ref[...] = jnp.zeros(acc_ref.shape, F32)
    to_smem.wait()

    lane = lax.broadcasted_iota(jnp.int32, (SEL_BLK, LANES), 1).astype(F32)
    row = lax.broadcasted_iota(jnp.int32, (SEL_BLK, LANES), 0).astype(F32)
    base = pl.program_id(0) * s

    def block(b, _):
        rows = pl.ds(b * SEL_BLK, SEL_BLK)
        tok = row + jnp.asarray(base + b * SEL_BLK, F32)
        for e in range(n_experts):
            col0 = jnp.minimum(start_smem[b, e] // LANES, n_col - 1)
            keyc = key_ref[rows, e:e + 1]
            affc = aff_ref[0, rows, e:e + 1]
            for j in range(2):
                hit = keyc == lane + jnp.asarray((col0 + j) * LANES, F32)
                acc_ref[0, e, col0 + j] += jnp.where(hit, tok, 0.0).reshape(SEL_BLK // 8, 8, LANES).sum(axis=0)
                acc_ref[1, e, col0 + j] += jnp.where(hit, affc, 0.0).reshape(SEL_BLK // 8, 8, LANES).sum(axis=0)
        return 0
    lax.fori_loop(0, n_blk, block, 0)

    for e in range(n_experts):
        for j in range(n_col):
            cols = slice(j * LANES, (j + 1) * LANES)
            idx_ref[0, e:e + 1, cols] = jnp.sum(acc_ref[0, e, j], axis=0, keepdims=True).astype(jnp.int32)
            gate_ref[0, e:e + 1, cols] = jnp.sum(acc_ref[1, e, j], axis=0, keepdims=True)


def _select(aff3, afft, cap):
    b, s, _ = aff3.shape
    n_experts = afft.shape[1]
    n_blk = s // SEL_BLK
    return pl.pallas_call(
        functools.partial(_select_kernel, n_experts=n_experts, cap=cap),
        grid=(b,),
        in_specs=[pl.BlockSpec((1, s, LANES), lambda i: (i, 0, 0)),
                  pl.BlockSpec((1, n_experts, s), lambda i: (i, 0, 0))],
        out_specs=[pl.BlockSpec((1, n_experts, cap), lambda i: (i, 0, 0))] * 2,
        out_shape=[jax.ShapeDtypeStruct((b, n_experts, cap), jnp.int32),
                   jax.ShapeDtypeStruct((b, n_experts, cap), F32)],
        scratch_shapes=[pltpu.VMEM((s, LANES), F32), pltpu.VMEM((s, LANES), F32),
                        pltpu.VMEM((s, LANES), F32),
                        pltpu.VMEM((n_blk, LANES), jnp.int32), pltpu.SMEM((n_blk, LANES), jnp.int32),
                        pltpu.VMEM((2, n_experts, cap // LANES + 1, 8, LANES), F32),
                        pltpu.SemaphoreType.DMA((1,))],
        compiler_params=_params(("arbitrary",)),
        name="moe_select",
    )(aff3, afft)


DOWN_COLS = 512
ROW_LOOKAHEAD = 2


def _moe_kernel(idx_ref, h_hbm, x_hbm, gate_ref, gprev_ref, wg_ref, wu_ref, wd_ref, o_hbm,
                xe32_ref, xe16_ref, y_ref, row_ref, sem_ref, bar_ref, *, rows, n_experts, n_f):
    del x_hbm
    e = pl.program_id(0)
    f = pl.program_id(1)
    ch = rows // n_f
    gather_sem, scatter_sem = sem_ref.at[0], sem_ref.at[1]
    acc_sem = lambda c: sem_ref.at[2 + c % 2]
    slot = e % 2
    pe = jnp.maximum(e - 1, 0)
    valid = jnp.where(e > 0, 1.0, 0.0).astype(F32)

    def h_row_in(t, i):
        return pltpu.make_async_copy(h_hbm.at[pl.ds(t, 1)], xe32_ref.at[pl.ds(i, 1)], gather_sem)

    def acc_row_in(c):
        return lambda t, i: pltpu.make_async_copy(o_hbm.at[pl.ds(t, 1)], row_ref.at[c, pl.ds(i, 1)],
                                                  acc_sem(c))

    def acc_row_out(c):
        return lambda t, i: pltpu.make_async_copy(row_ref.at[c, pl.ds(i, 1)], o_hbm.at[pl.ds(t, 1)],
                                                  scatter_sem)

    def start_unrolled(list_base, n, make, dst_base=0):
        for i in range(n):
            make(idx_ref[list_base + i], dst_base + i).start()

    def start_looped(list_base, n, make):
        def body(i, _):
            make(idx_ref[list_base + i], i).start()
            return 0
        lax.fori_loop(0, n, body, 0)

    def wait_rows(n_rows, sem):
        pltpu.make_async_copy(h_hbm.at[pl.ds(0, n_rows)], xe32_ref.at[pl.ds(0, n_rows)], sem).wait()

    def gated_rows(c, y_slot, g_ref, scale):
        crows = pl.ds(c * ch, ch)
        return row_ref[c] + y_ref[y_slot, crows, :] * (g_ref[0, crows, 0:1] * scale)

    @pl.when((e == 0) & (f == 0))
    def _():
        start_looped(0, rows, h_row_in)
        y_ref[1] = jnp.zeros(y_ref.shape[1:], F32)
        for c in range(ROW_LOOKAHEAD):
            start_looped(c * ch, ch, acc_row_in(c))

    def step(f):
        first, last = f == 0, f == n_f - 1
        wait_rows(ch, acc_sem(f))
        row_ref[f] = gated_rows(f, 1 - slot, gprev_ref, valid)
        pl.semaphore_signal(bar_ref, 1)
        pl.semaphore_wait(bar_ref, 1)
        if first:
            wait_rows(rows, gather_sem)
            xe16_ref[...] = xe32_ref[...].astype(BF16)
        start_unrolled(pe * rows + f * ch, ch, acc_row_out(f))

        xe = xe16_ref[...]
        g = _dot(xe, wg_ref[0].astype(BF16))
        start_unrolled(jnp.minimum(e + 1, n_experts - 1) * rows + f * ch, ch, h_row_in, f * ch)
        u = _dot(xe, wu_ref[0].astype(BF16))
        act = (g * (1.0 / (1.0 + jnp.exp(-g))) * u).astype(BF16)
        for n0 in range(0, y_ref.shape[2], DOWN_COLS):
            cols = slice(n0, n0 + DOWN_COLS)
            part = _dot(act, wd_ref[0, :, cols].astype(BF16))
            if first:
                y_ref[slot, :, cols] = part
            else:
                y_ref[slot, :, cols] += part

        if last:
            wait_rows(rows, scatter_sem)
            for c in range(ROW_LOOKAHEAD):
                start_unrolled(e * rows + c * ch, ch, acc_row_in(c))
        elif f + ROW_LOOKAHEAD < n_f:
            start_unrolled(pe * rows + (f + ROW_LOOKAHEAD) * ch, ch, acc_row_in(f + ROW_LOOKAHEAD))

    for fs in range(n_f):
        pl.when(f == fs)(functools.partial(step, fs))

    @pl.when((e == n_experts - 1) & (f == n_f - 1))
    def _():
        wait_rows(rows, gather_sem)
        for c in range(n_f):
            wait_rows(ch, acc_sem(c))
            if c + ROW_LOOKAHEAD < n_f:
                start_looped(e * rows + (c + ROW_LOOKAHEAD) * ch, ch, acc_row_in(c + ROW_LOOKAHEAD))
            row_ref[c] = gated_rows(c, slot, gate_ref, 1.0)
            start_looped(e * rows + c * ch, ch, acc_row_out(c))
        wait_rows(rows, scatter_sem)


def _moe_experts(idx_flat, gate_rows, h, x, w_gate, w_up, w_down, tf):
    t, d = x.shape
    ne, _, dff = w_gate.shape
    rows = idx_flat.shape[0] // ne
    n_f = dff // tf
    assert n_f >= 2
    grid_spec = pltpu.PrefetchScalarGridSpec(
        num_scalar_prefetch=1,
        grid=(ne, n_f),
        in_specs=[pl.BlockSpec(memory_space=pl.ANY),
                  pl.BlockSpec(memory_space=pl.ANY),
                  pl.BlockSpec((1, rows, LANES), lambda e, f, idx: (e, 0, 0)),
                  pl.BlockSpec((1, rows, LANES), lambda e, f, idx: (jnp.maximum(e - 1, 0), 0, 0)),
                  pl.BlockSpec((1, d, tf), lambda e, f, idx: (e, 0, f)),
                  pl.BlockSpec((1, d, tf), lambda e, f, idx: (e, 0, f)),
                  pl.BlockSpec((1, tf, d), lambda e, f, idx: (e, f, 0))],
        out_specs=pl.BlockSpec(memory_space=pl.ANY),
        scratch_shapes=[pltpu.VMEM((rows, d), F32), pltpu.VMEM((rows, d), BF16),
                        pltpu.VMEM((2, rows, d), F32), pltpu.VMEM((n_f, rows // n_f, d), F32),
                        pltpu.SemaphoreType.DMA((4,)), pltpu.SemaphoreType.REGULAR],
    )
    return pl.pallas_call(
        functools.partial(_moe_kernel, rows=rows, n_experts=ne, n_f=n_f),
        grid_spec=grid_spec,
        out_shape=jax.ShapeDtypeStruct((t, d), F32),
        input_output_aliases={2: 0},
        compiler_params=pltpu.CompilerParams(dimension_semantics=("arbitrary", "arbitrary"),
                                             vmem_limit_bytes=MOE_VMEM_LIMIT),
        name="moe_experts",
    )(idx_flat, h, x, gate_rows, gate_rows, w_gate, w_up, w_down)


def _moe(x, s, g_moe, w_router, w_gate, w_up, w_down):
    t, d = x.shape
    b = t // s
    ne = w_router.shape[1]
    cap = EC_CAPACITY * s // ne
    h, aff, afft = _router(x, s, g_moe, w_router, tm=512)
    idx, gate = _select(aff.reshape(b, s, LANES), afft, cap)
    idx_flat = idx.transpose(1, 0, 2).reshape(ne * b * cap)
    gate_rows = jnp.broadcast_to(gate.transpose(1, 0, 2).reshape(ne, b * cap, 1), (ne, b * cap, LANES))
    return _moe_experts(idx_flat, gate_rows, h, x, w_gate, w_up, w_down, tf=256)


def kernel(x, mem, positions, g_mix, w_in, g_qa, g_ka, g_qb, g_kb, sink_b, g_oa, g_ob, w_out,
           g_cross, g_mem, w_q_mem, w_kv_mem, g_qm, g_km, w_o_mem, g_moe, w_router, w_gate,
           w_up, w_down):
    b, s, d = x.shape
    t = b * s
    depth = g_mix.shape[0]
    width_a = N_HEADS_A * HEAD_DIM
    width_b = N_HEADS_B * HEAD_DIM
    kv_b = width_b // GQA_REP
    col = [0, width_a, 2 * width_a, 3 * width_a, 3 * width_a + width_b, 3 * width_a + width_b + kv_b]
    col = [c // HEAD_DIM for c in col]

    cos2, sin2 = _rope_tables(positions)
    xf = x.reshape(t, d)
    for l in range(depth):
        proj = _norm_matmul(xf, g_mix[l], w_in[l], tn=256)
        proj3 = proj.reshape(b, s, proj.shape[1])
        oa = _attn_a(proj3, cos2, sin2, g_qa[l], g_ka[l], col[0], col[1], col[2])
        ob = _attn_b(proj3, cos2, sin2, g_qb[l], g_kb[l], sink_b[l], col[3], col[4], col[5])
        xf = _out_proj(oa.reshape(t, width_a), ob.reshape(t, width_b), g_oa[l], g_ob[l], w_out[l],
                       xf, tm=2048, tn=256)
        kmem, vmem = _mem_kv(mem, g_mem[l], w_kv_mem[l], g_km[l])
        xf = _cross_attn(xf, g_cross[l], w_q_mem[l], g_qm[l], kmem, vmem, w_o_mem[l], s, tm=512)
        xf = _moe(xf, s, g_moe[l], w_router[l], w_gate[l], w_up[l], w_down[l])
    return xf.reshape(b, s, d)
```
